```python
import math
import jax, jax.numpy as jnp
from jax import lax
import numpy as np

D_MODEL = 1024
BATCH = 16
SEQ = 256
DEPTH = 2
DEC_BATCH = 4
DEC_SEQ = 2048
PAST_LEN = 256

GRID_W = 64
W_MIX = 512
N_BRANCH = 3
A_HEADS = 4
A_DK = W_MIX // A_HEADS
A_DV = W_MIX // A_HEADS
CHUNK = 64
B_CONV = 3
C_BLOCKS = 8
C_BLOCK_W = W_MIX // C_BLOCKS
C_CONV = 4
LRU_C = 8.0
N_EXPERTS = 32
TOP_K = 4
D_FF = 1024
SWIGLU_LIMIT = 7.0
SWIGLU_ALPHA = 1.702
N_IN_SPLIT = 10
IN_WIDTH = N_IN_SPLIT * W_MIX
N_MOD = 6
EPS = 1e-6

kernel_name = 'hybrid_hgrn2_shortconv_rglru_moe_diffusion_step'


def rmsnorm(x, w):
    x32 = x.astype(jnp.float32)
    y = x32 * lax.rsqrt(jnp.mean(x32 * x32, axis=-1, keepdims=True) + EPS)
    return (y * w.astype(jnp.float32)).astype(x.dtype)


def dwconv(u, w, left, grid):
    b, t, ch = u.shape
    k = w.shape[0]
    if grid:
        rows = t // GRID_W
        u = u.reshape(b, rows, GRID_W, ch)
    length = u.shape[-2]
    pad = [(0, 0)] * (u.ndim - 2) + [(left, k - 1 - left), (0, 0)]
    up = jnp.pad(u, pad)
    out = w[0] * up[..., 0:length, :]
    for j in range(1, k):
        out = out + w[j] * up[..., j:j + length, :]
    return out.reshape(b, t, ch)


def gla_chunked(q, k, g, v, s0):
    b, h, t, dk = q.shape
    dv = v.shape[-1]
    n = t // CHUNK

    def to_chunks(a):
        return jnp.moveaxis(a.reshape(b, h, n, CHUNK, a.shape[-1]), 2, 0)

    tri = jnp.tril(jnp.ones((CHUNK, CHUNK), dtype=bool))

    def step(s, xs):
        qc, kc, gc, vc = xs
        bc = jnp.cumsum(gc, axis=2)
        inter = jnp.einsum('bhtk,bhkv->bhtv', qc * jnp.exp(bc), s)
        diff = bc[:, :, :, None, :] - bc[:, :, None, :, :]
        decay = jnp.where(tri[:, :, None], jnp.exp(jnp.minimum(diff, 0.0)), 0.0)
        att = jnp.einsum('bhtk,bhtsk,bhsk->bhts', qc, decay, kc)
        intra = jnp.einsum('bhts,bhsv->bhtv', att, vc)
        blast = bc[:, :, -1:, :]
        s_new = jnp.exp(blast[:, :, 0, :])[..., None] * s + jnp.einsum(
            'bhsk,bhsv->bhkv', kc * jnp.exp(blast - bc), vc)
        return s_new, inter + intra

    s_fin, o = lax.scan(step, s0, (to_chunks(q), to_chunks(k), to_chunks(g), to_chunks(v)))
    o = jnp.moveaxis(o, 0, 2).reshape(b, h, t, dv)
    return o, s_fin


def lru_scan(a, x, h0, reverse):
    def combine(e1, e2):
        a1, b1 = e1
        a2, b2 = e2
        return a1 * a2, a2 * b1 + b2
    acum, bcum = lax.associative_scan(combine, (a, x), axis=1, reverse=reverse)
    return acum * h0[:, None, :] + bcum


def _heads(a, b, t):
    return a.reshape(b, t, A_HEADS, -1).transpose(0, 2, 1, 3).astype(jnp.float32)


def hgrn2_branch(q, f_fw, f_bw, v, g_out, lb, norm_w, s0):
    b, t, _ = q.shape
    qh = jax.nn.silu(_heads(q, b, t)) * (A_DK ** -0.5)
    vh = _heads(v, b, t)
    log_lb = jnp.log(lb)
    log_1mlb = jnp.log1p(-lb)
    shp = (A_HEADS, 1, A_DK)
    outs, finals = [], []
    for d, (fa, rev) in enumerate(((f_fw, False), (f_bw, True))):
        fh = _heads(fa, b, t)
        k = (1.0 - lb[d]).reshape(shp) * jax.nn.sigmoid(-fh)
        g = jnp.logaddexp(log_lb[d].reshape(shp),
                          log_1mlb[d].reshape(shp) + jax.nn.log_sigmoid(fh))
        seqs = (qh, k, g, vh)
        if rev:
            seqs = tuple(jnp.flip(a, axis=2) for a in seqs)
        o, s_fin = gla_chunked(seqs[0], seqs[1], seqs[2], seqs[3], s0[:, d].astype(jnp.float32))
        if rev:
            o = jnp.flip(o, axis=2)
        outs.append(o)
        finals.append(s_fin)
    o = (outs[0] + outs[1]).transpose(0, 2, 1, 3)
    o = rmsnorm(o, norm_w.reshape(A_HEADS, A_DV))
    gate = jax.nn.silu(g_out.reshape(b, t, A_HEADS, A_DV))
    y = (o.astype(q.dtype) * gate).reshape(b, t, W_MIX)
    return y, jnp.stack(finals, axis=1)


def rglru_branch(rx, rg, conv_w, conv_b, wa, ba, wi, bi, lam, s0, grid):
    b, t, _ = rx.shape
    xr = (dwconv(rx, conv_w, 2, grid) + conv_b).astype(jnp.float32)
    xb = xr.reshape(b, t, C_BLOCKS, C_BLOCK_W)
    hs, finals = [], []
    for d, rev in ((0, False), (1, True)):
        r = jax.nn.sigmoid(jnp.einsum('btnk,nkj->btnj', xb, wa[d].astype(jnp.float32)).reshape(b, t, W_MIX) + ba[d])
        i = jax.nn.sigmoid(jnp.einsum('btnk,nkj->btnj', xb, wi[d].astype(jnp.float32)).reshape(b, t, W_MIX) + bi[d])
        log_a = -LRU_C * r * jax.nn.softplus(-lam[d].astype(jnp.float32))
        mult = jnp.sqrt(-jnp.expm1(2.0 * log_a))
        hd = lru_scan(jnp.exp(log_a), mult * i * xr, s0[:, d].astype(jnp.float32), rev)
        hs.append(hd)
        finals.append(hd[:, 0] if rev else hd[:, -1])
    y = (hs[0] + hs[1]).astype(rx.dtype) * jax.nn.gelu(rg)
    return y, jnp.stack(finals, axis=1)


def token_mixer(h, lp, s_hgrn0, s_lru0, grid):
    b, t, _ = h.shape
    p = h @ lp['w_in']
    q, f_fw, f_bw, v, g_out, sb, sc, sx, rx, rg = jnp.split(p, N_IN_SPLIT, axis=-1)
    ya, s_hgrn = hgrn2_branch(q, f_fw, f_bw, v, g_out, lp['hgrn_lb'], lp['hgrn_norm'], s_hgrn0)
    yb = sb * dwconv(sc * sx, lp['sconv_w'], 1, grid)
    yc, s_lru = rglru_branch(rx, rg, lp['lru_conv_w'], lp['lru_conv_b'], lp['lru_wa'], lp['lru_ba'],
                             lp['lru_wi'], lp['lru_bi'], lp['lru_lambda'], s_lru0, grid)
    wb = lp['w_branch']
    gates = jax.nn.sigmoid(h @ lp['w_merge'] + lp['b_merge']).reshape(b, t, N_BRANCH, D_MODEL)
    merged = gates[:, :, 0] * (ya @ wb[0]) + gates[:, :, 1] * (yb @ wb[1]) + gates[:, :, 2] * (yc @ wb[2])
    return merged @ lp['w_out'], s_hgrn, s_lru


def moe(h, lp):
    b, t, d = h.shape
    xt = h.reshape(b * t, d)
    logits = (xt @ lp['w_router'] + lp['b_router']).astype(jnp.float32)
    top_v, top_i = lax.top_k(logits, TOP_K)
    wts = jax.nn.softmax(top_v, axis=-1)
    comb = jnp.einsum('nk,nke->en', wts, jax.nn.one_hot(top_i, N_EXPERTS, dtype=jnp.float32)).astype(h.dtype)

    def expert(acc, ep):
        wgu, bgu, wd, bd, cw = ep
        gu = xt @ wgu + bgu
        glu, lin = jnp.split(gu, 2, axis=-1)
        glu = jnp.minimum(glu, SWIGLU_LIMIT)
        lin = jnp.clip(lin, -SWIGLU_LIMIT, SWIGLU_LIMIT)
        act = glu * jax.nn.sigmoid(SWIGLU_ALPHA * glu) * (lin + 1.0)
        return (acc + cw[:, None] * (act @ wd + bd)).astype(acc.dtype), None

    acc, _ = lax.scan(expert, jnp.zeros_like(xt),
                      (lp['w_gate_up'], lp['b_gate_up'], lp['w_down'], lp['b_down'], comb))
    return acc.reshape(b, t, d)


def layer(x, mod, lp, s_hgrn0, s_lru0, grid):
    shift1, scale1, gate1, shift2, scale2, gate2 = jnp.split(mod, N_MOD, axis=-1)
    h = rmsnorm(x, lp['norm_mix']) * (1.0 + scale1) + shift1
    y, s_hgrn, s_lru = token_mixer(h, lp, s_hgrn0, s_lru0, grid)
    x = x + gate1 * y
    h = rmsnorm(x, lp['norm_ffn']) * (1.0 + scale2) + shift2
    x = x + gate2 * moe(h, lp)
    return x, s_hgrn, s_lru


def setup_inputs(seed: int = 0) -> dict:
    key = jax.random.key(seed)
    ks = jax.random.split(key, 40)
    f32 = jnp.float32
    nrm = lambda k, shape, s: jax.random.normal(k, shape, f32) * s
    dsc = D_MODEL ** -0.5
    u = jax.random.uniform(ks[20], (DEPTH, 2, W_MIX), f32, minval=0.9, maxval=0.999)
    sa = u ** (1.0 / LRU_C)
    lam = jnp.log(sa) - jnp.log1p(-sa)
    return {
        'x_prompt': nrm(ks[0], (BATCH, SEQ, D_MODEL), 1.0),
        'x_sample': nrm(ks[1], (DEC_BATCH, DEC_SEQ, D_MODEL), 1.0),
        'state_hgrn': nrm(ks[2], (DEC_BATCH, DEPTH, 2, A_HEADS, A_DK, A_DV), 0.5),
        'state_rglru': nrm(ks[3], (DEC_BATCH, DEPTH, 2, W_MIX), 0.5),
        'c': nrm(ks[4], (DEC_BATCH, D_MODEL), 1.0),
        'c_ctx': nrm(ks[5], (D_MODEL,), 1.0),
        'norm_mix_w': 1.0 + nrm(ks[6], (DEPTH, D_MODEL), 0.02),
        'norm_ffn_w': 1.0 + nrm(ks[7], (DEPTH, D_MODEL), 0.02),
        'norm_final_w': 1.0 + nrm(ks[8], (D_MODEL,), 0.02),
        'w_mod': nrm(ks[9], (DEPTH, D_MODEL, N_MOD * D_MODEL), dsc),
        'b_mod': nrm(ks[10], (DEPTH, N_MOD * D_MODEL), 0.02),
        'w_in': nrm(ks[11], (DEPTH, D_MODEL, IN_WIDTH), dsc),
        'hgrn_lb_logits': nrm(ks[12], (DEPTH, 2, W_MIX), 0.5),
        'hgrn_norm_w': 1.0 + nrm(ks[13], (DEPTH, W_MIX), 0.02),
        'sconv_w': nrm(ks[14], (DEPTH, B_CONV, W_MIX), B_CONV ** -0.5),
        'lru_conv_w': nrm(ks[15], (DEPTH, C_CONV, W_MIX), C_CONV ** -0.5),
        'lru_conv_b': nrm(ks[16], (DEPTH, W_MIX), 0.02),
        'lru_wa': nrm(ks[17], (DEPTH, 2, C_BLOCKS, C_BLOCK_W, C_BLOCK_W), C_BLOCK_W ** -0.5),
        'lru_ba': nrm(ks[18], (DEPTH, 2, W_MIX), 0.02),
        'lru_wi': nrm(ks[19], (DEPTH, 2, C_BLOCKS, C_BLOCK_W, C_BLOCK_W), C_BLOCK_W ** -0.5),
        'lru_bi': nrm(ks[21], (DEPTH, 2, W_MIX), 0.02),
        'lru_lambda': lam,
        'w_branch': nrm(ks[22], (DEPTH, N_BRANCH, W_MIX, D_MODEL), W_MIX ** -0.5),
        'w_merge': nrm(ks[23], (DEPTH, D_MODEL, N_BRANCH * D_MODEL), dsc),
        'b_merge': nrm(ks[24], (DEPTH, N_BRANCH * D_MODEL), 0.02),
        'w_out': nrm(ks[25], (DEPTH, D_MODEL, D_MODEL), dsc),
        'w_router': nrm(ks[26], (DEPTH, D_MODEL, N_EXPERTS), dsc),
        'b_router': nrm(ks[27], (DEPTH, N_EXPERTS), 0.01),
        'w_gate_up': nrm(ks[28], (DEPTH, N_EXPERTS, D_MODEL, 2 * D_FF), dsc),
        'b_gate_up': nrm(ks[29], (DEPTH, N_EXPERTS, 2 * D_FF), 0.02),
        'w_down': nrm(ks[30], (DEPTH, N_EXPERTS, D_FF, D_MODEL), D_FF ** -0.5),
        'b_down': nrm(ks[31], (DEPTH, N_EXPERTS, D_MODEL), 0.02),
    }


def reference(x_prompt, x_sample, state_hgrn, state_rglru, c, c_ctx,
              norm_mix_w, norm_ffn_w, norm_final_w, w_mod, b_mod, w_in,
              hgrn_lb_logits, hgrn_norm_w, sconv_w, lru_conv_w, lru_conv_b,
              lru_wa, lru_ba, lru_wi, lru_bi, lru_lambda, w_branch, w_merge, b_merge,
              w_out, w_router, b_router, w_gate_up, b_gate_up, w_down, b_down):
    f32 = jnp.float32
    lb_cum = jnp.cumsum(jax.nn.softmax(hgrn_lb_logits.astype(f32), axis=0), axis=0)
    lower_bounds = lb_cum - lb_cum[0:1]
    bp = x_prompt.shape[0]
    zero_hgrn = jnp.zeros((bp, 2, A_HEADS, A_DK, A_DV), f32)
    zero_lru = jnp.zeros((bp, 2, W_MIX), f32)
    xp = x_prompt
    xs = x_sample
    new_hgrn, new_lru = [], []
    for l in range(DEPTH):
        lp = dict(norm_mix=norm_mix_w[l], norm_ffn=norm_ffn_w[l], w_in=w_in[l],
                  hgrn_lb=lower_bounds[l], hgrn_norm=hgrn_norm_w[l], sconv_w=sconv_w[l],
                  lru_conv_w=lru_conv_w[l], lru_conv_b=lru_conv_b[l], lru_wa=lru_wa[l],
                  lru_ba=lru_ba[l], lru_wi=lru_wi[l], lru_bi=lru_bi[l], lru_lambda=lru_lambda[l],
                  w_branch=w_branch[l], w_merge=w_merge[l], b_merge=b_merge[l], w_out=w_out[l],
                  w_router=w_router[l], b_router=b_router[l], w_gate_up=w_gate_up[l],
                  b_gate_up=b_gate_up[l], w_down=w_down[l], b_down=b_down[l])
        mod_ctx = (jax.nn.silu(c_ctx) @ w_mod[l] + b_mod[l])[None, None, :]
        mod_lat = (jax.nn.silu(c) @ w_mod[l] + b_mod[l])[:, None, :]
        xp, s_h, s_l = layer(xp, mod_ctx, lp, zero_hgrn, zero_lru, False)
        new_hgrn.append(s_h)
        new_lru.append(s_l)
        xs, _, _ = layer(xs, mod_lat, lp, state_hgrn[:, l], state_rglru[:, l], True)
    y_prompt = rmsnorm(xp, norm_final_w)
    y_sample = rmsnorm(xs, norm_final_w)
    new_state_hgrn = jnp.stack(new_hgrn, axis=1).astype(x_prompt.dtype)
    new_state_rglru = jnp.stack(new_lru, axis=1).astype(x_prompt.dtype)
    return (y_prompt, y_sample, new_state_hgrn, new_state_rglru)
```

```python
import functools

import jax
import jax.numpy as jnp
from jax import lax
from jax.experimental import pallas as pl
from jax.experimental.pallas import tpu as pltpu

F32 = jnp.float32
BF16 = jnp.bfloat16
I32 = jnp.int32

D_MODEL = 1024
W_MIX = 512
N_HEADS = 4
D_HEAD = 128
N_IN = 10
N_MOD = 6
EPS = 1e-6
LRU_C = 8.0
N_EXPERTS = 32
TOP_K = 4
D_FF = 1024
SWIGLU_LIMIT = 7.0
SWIGLU_ALPHA = 1.702

TM = 256
CHUNK = 64
SUB = 16
N_SUB = CHUNK // SUB
SUB_SHIFT = SUB.bit_length() - 1
N_CHUNK = TM // CHUNK
GRID_W = 64
MOD_ROWS = 8
VMEM_LIMIT = 56 * 1024 * 1024


def _dot(a, b):
    return jnp.dot(a, b, preferred_element_type=F32)


def _dot_nt(a, b):
    return lax.dot_general(a, b, (((1,), (1,)), ((), ())), preferred_element_type=F32)


def _sigmoid(x):
    return 1.0 / (1.0 + jnp.exp(-x))


def _rms(x, w):
    return x * lax.rsqrt(jnp.mean(x * x, axis=-1, keepdims=True) + EPS) * w


def _split3(x):
    a = x.astype(BF16)
    r = x - a.astype(F32)
    b = r.astype(BF16)
    c = (r - b.astype(F32)).astype(BF16)
    return a, b, c


def _const_spec(shape):
    n = len(shape)
    return pl.BlockSpec(shape, lambda *_: (0,) * n, pipeline_mode=pl.Buffered(1))


def _mod_kernel(c_ref, w_ref, b_ref, o_ref):
    c = c_ref[...]
    s = (c * _sigmoid(c)).astype(BF16)
    o_ref[0] = _dot(s, w_ref[0].astype(BF16)) + b_ref[0]


def _modulation(cc, w_mod, b_mod):
    depth, _, width = w_mod.shape
    nb = 1536
    return pl.pallas_call(
        _mod_kernel,
        grid=(depth, width // nb),
        in_specs=[
            pl.BlockSpec((MOD_ROWS, D_MODEL), lambda l, j: (0, 0)),
            pl.BlockSpec((1, D_MODEL, nb), lambda l, j: (l, 0, j)),
            pl.BlockSpec((1, 1, nb), lambda l, j: (l, 0, j)),
        ],
        out_specs=pl.BlockSpec((1, MOD_ROWS, nb), lambda l, j: (l, 0, j)),
        out_shape=jax.ShapeDtypeStruct((depth, MOD_ROWS, width), F32),
        compiler_params=pltpu.CompilerParams(
            dimension_semantics=("arbitrary", "arbitrary"), vmem_limit_bytes=VMEM_LIMIT),
        name="modulation",
    )(cc, w_mod, b_mod.reshape(depth, 1, width))


def _premix_kernel(tmod_ref, x_ref, mod_ref, nw_ref, win_ref, p_ref):
    del tmod_ref
    mod = mod_ref[0]
    h = _rms(x_ref[...], nw_ref[...]) * (1.0 + mod[:, D_MODEL:2 * D_MODEL]) + mod[:, 0:D_MODEL]
    p_ref[...] = _dot(h.astype(BF16), win_ref[...])


def _premix(x, tile_mod, mod, norm_w, w_in_bf):
    n = x.shape[0]
    width = w_in_bf.shape[1]
    grid_spec = pltpu.PrefetchScalarGridSpec(
        num_scalar_prefetch=1,
        grid=(n // TM,),
        in_specs=[
            pl.BlockSpec((TM, D_MODEL), lambda i, tm: (i, 0)),
            pl.BlockSpec((1, 1, N_MOD * D_MODEL), lambda i, tm: (tm[i], 0, 0)),
            _const_spec((1, D_MODEL)),
            _const_spec((D_MODEL, width)),
        ],
        out_specs=pl.BlockSpec((TM, width), lambda i, tm: (i, 0)),
    )
    return pl.pallas_call(
        _premix_kernel,
        grid_spec=grid_spec,
        out_shape=jax.ShapeDtypeStruct((n, width), F32),
        compiler_params=pltpu.CompilerParams(
            dimension_semantics=("arbitrary",), vmem_limit_bytes=VMEM_LIMIT),
        name="premix",
    )(tile_mod, x, mod, norm_w, w_in_bf)


def _shift_rows(x, s):
    return pltpu.roll(x, s % x.shape[0], axis=0)


def _lru_scan(a, b, reverse):
    n = a.shape[0]
    row = lax.broadcasted_iota(I32, a.shape, 0)
    s = 1
    while s < n:
        if reverse:
            keep = row < n - s
            a_sh = jnp.where(keep, _shift_rows(a, -s), 1.0)
            b_sh = jnp.where(keep, _shift_rows(b, -s), 0.0)
        else:
            keep = row >= s
            a_sh = jnp.where(keep, _shift_rows(a, s), 1.0)
            b_sh = jnp.where(keep, _shift_rows(b, s), 0.0)
        b = a * b_sh + b
        a = a * a_sh
        s *= 2
    return a, b


def _scan_kernel(tf_ref, tb_ref, first_ref, last_ref, sb_ref, lm1_ref,
                 qf_ref, ff_ref, vf_ref, rxf_ref, qb_ref, fb_ref, vb_ref, rxb_ref,
                 lb_ref, lp_ref, wlru_ref, s0h_ref, s0l_ref,
                 of_ref, hlf_ref, ob_ref, hlb_ref, sfh_ref, sfl_ref,
                 st_ref, hcar_ref, qs_ref, ks_ref, gs_ref, lcs_ref, kcs_ref):
    del tf_ref, tb_ref, sb_ref
    step = pl.program_id(0)

    @pl.when(first_ref[step] == 1)
    def _init():
        for d in range(2):
            for h in range(N_HEADS):
                st_ref[d, h] = s0h_ref[0, d, h].T
        hcar_ref[0:2, :] = s0l_ref[0]

    lm1 = lm1_ref[step]
    row = lax.broadcasted_iota(I32, (TM, W_MIX), 0)
    pos = row & lm1

    dirs = ((qf_ref, ff_ref, rxf_ref, hlf_ref), (qb_ref, fb_ref, rxb_ref, hlb_ref))
    for d, (q_ref, f_ref, rx_ref, hl_ref) in enumerate(dirs):
        q = q_ref[...]
        qs_ref[d] = q * _sigmoid(q) * (D_HEAD ** -0.5)
        f = f_ref[...]
        e = jnp.exp(-jnp.abs(f))
        r = 1.0 / (1.0 + e)
        sig_neg = jnp.where(f >= 0, e * r, r)
        log_sig = jnp.minimum(f, 0.0) - jnp.log1p(e)
        lb = lb_ref[d:d + 1, :]
        ks_ref[d] = (1.0 - lb) * sig_neg
        la = jnp.log(lb)
        lbb = jnp.log1p(-lb) + log_sig
        gs_ref[d] = jnp.maximum(la, lbb) + jnp.log1p(jnp.exp(-jnp.abs(la - lbb)))

        u = rx_ref[...]
        xr = (lp_ref[2:3, :] * u + lp_ref[4:5, :]
              + lp_ref[0:1, :] * jnp.where(pos >= 2, _shift_rows(u, 2), 0.0)
              + lp_ref[1:2, :] * jnp.where(pos >= 1, _shift_rows(u, 1), 0.0)
              + lp_ref[3:4, :] * jnp.where(pos < lm1, _shift_rows(u, -1), 0.0))
        rl, il = [], []
        for nblk in range(W_MIX // 128):
            z = _dot(xr[:, 128 * nblk:128 * (nblk + 1)].astype(BF16), wlru_ref[d, nblk])
            rl.append(z[:, :128])
            il.append(z[:, 128:])
        r_gate = _sigmoid(jnp.concatenate(rl, axis=1) + lp_ref[5 + d:6 + d, :])
        i_gate = _sigmoid(jnp.concatenate(il, axis=1) + lp_ref[7 + d:8 + d, :])
        lam = lp_ref[9 + d:10 + d, :]
        softplus = jnp.maximum(-lam, 0.0) + jnp.log1p(jnp.exp(-jnp.abs(lam)))
        log_a = (-LRU_C) * r_gate * softplus
        a = jnp.exp(log_a)
        th = jnp.tanh(log_a)
        bx = jnp.sqrt(-2.0 * th / (1.0 - th)) * i_gate * xr
        a_cum, b_cum = _lru_scan(a, bx, reverse=(d == 1))
        hd = a_cum * hcar_ref[d:d + 1, :] + b_cum
        hl_ref[...] = hd
        edge = TM - 1 if d == 0 else 0
        hcar_ref[d:d + 1, :] = hd[edge:edge + 1, :]

    r64 = lax.broadcasted_iota(I32, (CHUNK, CHUNK), 0)
    c64 = lax.broadcasted_iota(I32, (CHUNK, CHUNK), 1)
    same_sub = (r64 >> SUB_SHIFT) == (c64 >> SUB_SHIFT)
    rel = c64 - ((r64 >> SUB_SHIFT) << SUB_SHIFT)
    rmod = r64 & (SUB - 1)
    sub_of_row = lax.broadcasted_iota(I32, (CHUNK, W_MIX), 0) >> SUB_SHIFT
    ones_red = jnp.ones((D_HEAD, CHUNK), BF16)
    v_refs = (vf_ref, vb_ref)
    o_refs = (of_ref, ob_ref)

    def chunk_body(ci, carry):
        for d in range(2):
            rev = d == 1
            c0 = pl.multiple_of((N_CHUNK - 1 - ci) * CHUNK if rev else ci * CHUNK, CHUNK)
            order = tuple(reversed(range(N_SUB))) if rev else tuple(range(N_SUB))
            q = qs_ref[d, pl.ds(c0, CHUNK), :]
            k = ks_ref[d, pl.ds(c0, CHUNK), :]
            g = gs_ref[d, pl.ds(c0, CHUNK), :]
            v = v_refs[d][pl.ds(c0, CHUNK), :]

            tri = same_sub & ((c64 >= r64) if rev else (c64 <= r64))
            sel = jnp.concatenate([tri.astype(BF16), same_sub.astype(BF16)], axis=0)
            res = _dot(sel, jnp.concatenate(_split3(g), axis=1))
            res = res[:, 0:W_MIX] + res[:, W_MIX:2 * W_MIX] + res[:, 2 * W_MIX:3 * W_MIX]
            lc = res[0:CHUNK]
            totb = res[CHUNK:2 * CHUNK]
            lcs_ref[d] = lc
            kcs_ref[d] = k

            qh = q * jnp.exp(lc)
            kt = k * jnp.exp(totb - lc)

            tot = [totb[SUB * a:SUB * a + 1, :] for a in order]
            cum = [tot[0]]
            for i in range(1, N_SUB):
                cum.append(cum[-1] + tot[i])
            masks = [sub_of_row == a for a in order]

            def rows_select(vals):
                out = vals[-1]
                for m, val in zip(reversed(masks[:-1]), reversed(vals[:-1])):
                    out = jnp.where(m, val, out)
                return out

            one = jnp.ones((1, W_MIX), F32)
            zero = jnp.zeros((1, W_MIX), F32)
            qe = qh * rows_select([one] + [jnp.exp(cum[i - 1]) for i in range(1, N_SUB)])
            kl = kt * rows_select([jnp.exp(cum[-1] - cum[j]) for j in range(N_SUB - 1)] + [one])
            chunk_decay = jnp.exp(cum[-1])
            lhs, kts = [], []
            for j in range(N_SUB - 1):
                fac = [zero] * (j + 1) + [one] + [jnp.exp(cum[i - 1] - cum[j]) for i in range(j + 2, N_SUB)]
                lhs.append((qh * rows_select(fac)).astype(BF16))
                kts.append(jnp.where(masks[j], kt, 0.0).astype(BF16))

            att_d = [jnp.zeros((CHUNK, CHUNK), F32) for _ in range(N_HEADS)]
            for s in range(SUB):
                ksel = jnp.concatenate(
                    [jnp.broadcast_to(kcs_ref[d, SUB * a + s:SUB * a + s + 1, :], (SUB, W_MIX))
                     for a in range(N_SUB)], axis=0)
                lsel = jnp.concatenate(
                    [jnp.broadcast_to(lcs_ref[d, SUB * a + s:SUB * a + s + 1, :], (SUB, W_MIX))
                     for a in range(N_SUB)], axis=0)
                z = (q * ksel * jnp.exp(jnp.minimum(lc - lsel, 0.0))).astype(BF16)
                dmask = (rel == s) & ((rmod <= s) if rev else (rmod >= s))
                for h in range(N_HEADS):
                    col = _dot(z[:, D_HEAD * h:D_HEAD * (h + 1)], ones_red)
                    att_d[h] = jnp.where(dmask, col, att_d[h])

            outs = []
            for h in range(N_HEADS):
                sl = slice(D_HEAD * h, D_HEAD * (h + 1))
                att = att_d[h]
                for j in range(N_SUB - 1):
                    att = att + _dot_nt(lhs[j][:, sl], kts[j][:, sl])
                st = st_ref[d, h]
                vh = v[:, sl]
                o = _dot_nt(qe[:, sl].astype(BF16), st.astype(BF16)) + _dot(att.astype(BF16), vh.astype(BF16))
                outs.append(o)
                st_ref[d, h] = st * chunk_decay[:, sl] + _dot(vh.T.astype(BF16), kl[:, sl].astype(BF16))
            o_refs[d][pl.ds(c0, CHUNK), :] = jnp.concatenate(outs, axis=1)
        return carry

    lax.fori_loop(0, N_CHUNK, chunk_body, 0)

    @pl.when(last_ref[step] == 1)
    def _final():
        for d in range(2):
            for h in range(N_HEADS):
                sfh_ref[0, d, h] = st_ref[d, h].T
        sfl_ref[0] = hcar_ref[0:2, :]


def _scans(p, tabs, lb, lru_par, w_lru, s0_hgrn, s0_lru):
    n = p.shape[0]
    n_steps = tabs[0].shape[0]
    n_seq = s0_hgrn.shape[0]

    def col(c, which):
        if which == 0:
            return pl.BlockSpec((TM, W_MIX), lambda s, tf, tb, fi, la, sb, lm: (tf[s], c))
        return pl.BlockSpec((TM, W_MIX), lambda s, tf, tb, fi, la, sb, lm: (tb[s], c))

    def out_col(which):
        if which == 0:
            return pl.BlockSpec((TM, W_MIX), lambda s, tf, tb, fi, la, sb, lm: (tf[s], 0))
        return pl.BlockSpec((TM, W_MIX), lambda s, tf, tb, fi, la, sb, lm: (tb[s], 0))

    st_spec = pl.BlockSpec((1, 2, N_HEADS, D_HEAD, D_HEAD),
                           lambda s, tf, tb, fi, la, sb, lm: (sb[s], 0, 0, 0, 0))
    sl_spec = pl.BlockSpec((1, 2, W_MIX), lambda s, tf, tb, fi, la, sb, lm: (sb[s], 0, 0))
    grid_spec = pltpu.PrefetchScalarGridSpec(
        num_scalar_prefetch=6,
        grid=(n_steps,),
        in_specs=[col(0, 0), col(1, 0), col(3, 0), col(8, 0),
                  col(0, 1), col(2, 1), col(3, 1), col(8, 1),
                  _const_spec((2, W_MIX)), _const_spec((16, W_MIX)),
                  _const_spec((2, W_MIX // 128, 128, 256)),
                  st_spec, sl_spec],
        out_specs=[out_col(0), out_col(0), out_col(1), out_col(1), st_spec, sl_spec],
        scratch_shapes=[
            pltpu.VMEM((2, N_HEADS, D_HEAD, D_HEAD), F32),
            pltpu.VMEM((8, W_MIX), F32),
            pltpu.VMEM((2, TM, W_MIX), F32),
            pltpu.VMEM((2, TM, W_MIX), F32),
            pltpu.VMEM((2, TM, W_MIX), F32),
            pltpu.VMEM((2, CHUNK, W_MIX), F32),
            pltpu.VMEM((2, CHUNK, W_MIX), F32),
        ],
    )
    row = jax.ShapeDtypeStruct((n, W_MIX), F32)
    return pl.pallas_call(
        _scan_kernel,
        grid_spec=grid_spec,
        out_shape=[row, row, row, row,
                   jax.ShapeDtypeStruct((n_seq, 2, N_HEADS, D_HEAD, D_HEAD), F32),
                   jax.ShapeDtypeStruct((n_seq, 2, W_MIX), F32)],
        compiler_params=pltpu.CompilerParams(
            dimension_semantics=("arbitrary",), vmem_limit_bytes=VMEM_LIMIT),
        name="scans",
    )(*tabs, p, p, p, p, p, p, p, p, lb, lru_par, w_lru, s0_hgrn, s0_lru)


def _merge_kernel(n_ctx_tiles, tmod_ref,
                  x_ref, go_ref, sb_ref, sc_ref, sx_ref, rg_ref, of_ref, ob_ref, hlf_ref, hlb_ref,
                  mod_ref, nmix_ref, wmerge_ref, bmerge_ref, wbr_ref, wout_ref, hnorm_ref, scw_ref,
                  nffn_ref, wrt_ref, br_ref,
                  x1_ref, h2_ref, ri_ref, rw_ref, cnt_ref, base_ref):
    del tmod_ref
    i = pl.program_id(0)

    @pl.when(i == 0)
    def _init():
        base_ref[...] = jnp.zeros_like(base_ref)

    mod = mod_ref[0]
    x = x_ref[...]
    h = _rms(x, nmix_ref[...]) * (1.0 + mod[:, D_MODEL:2 * D_MODEL]) + mod[:, 0:D_MODEL]
    gates = _sigmoid(_dot(h.astype(BF16), wmerge_ref[...]) + bmerge_ref[...])

    o = of_ref[...] + ob_ref[...]
    go = go_ref[...]
    parts = []
    for hh in range(N_HEADS):
        sl = slice(D_HEAD * hh, D_HEAD * (hh + 1))
        parts.append(_rms(o[:, sl], hnorm_ref[:, sl]))
    ya = jnp.concatenate(parts, axis=1) * (go * _sigmoid(go))

    lm1 = jnp.where(i < n_ctx_tiles, TM - 1, GRID_W - 1)
    pos = lax.broadcasted_iota(I32, (TM, W_MIX), 0) & lm1
    u = sc_ref[...] * sx_ref[...]
    conv = (scw_ref[1:2, :] * u
            + scw_ref[0:1, :] * jnp.where(pos >= 1, _shift_rows(u, 1), 0.0)
            + scw_ref[2:3, :] * jnp.where(pos < lm1, _shift_rows(u, -1), 0.0))
    yb = sb_ref[...] * conv

    yc = (hlf_ref[...] + hlb_ref[...]) * jax.nn.gelu(rg_ref[...])

    merged = (gates[:, 0:D_MODEL] * _dot(ya.astype(BF16), wbr_ref[0])
              + gates[:, D_MODEL:2 * D_MODEL] * _dot(yb.astype(BF16), wbr_ref[1])
              + gates[:, 2 * D_MODEL:3 * D_MODEL] * _dot(yc.astype(BF16), wbr_ref[2]))
    y = _dot(merged.astype(BF16), wout_ref[...])
    x1 = x + mod[:, 2 * D_MODEL:3 * D_MODEL] * y
    x1_ref[...] = x1

    h2 = _rms(x1, nffn_ref[...]) * (1.0 + mod[:, 4 * D_MODEL:5 * D_MODEL]) + mod[:, 3 * D_MODEL:4 * D_MODEL]
    h2_ref[...] = h2

    h_hi = h2.astype(BF16)
    h_lo = (h2 - h_hi.astype(F32)).astype(BF16)
    wr = wrt_ref[...]
    w_hi = wr.astype(BF16)
    w_lo = (wr - w_hi.astype(F32)).astype(BF16)
    logits = _dot_nt(w_hi, h_hi) + _dot_nt(w_hi, h_lo) + _dot_nt(w_lo, h_hi) + br_ref[...]

    erow = lax.broadcasted_iota(I32, (N_EXPERTS, TM), 0).astype(F32)
    r8 = lax.broadcasted_iota(I32, (8, TM), 0)
    tri = (lax.broadcasted_iota(I32, (TM, TM), 0) <= lax.broadcasted_iota(I32, (TM, TM), 1)).astype(BF16)
    base = base_ref[...]
    top_v, hots, idxs = [], [], []
    work = logits
    for _ in range(TOP_K):
        m = jnp.max(work, axis=0, keepdims=True)
        idx = jnp.min(jnp.where(work == m, erow, float(N_EXPERTS)), axis=0, keepdims=True)
        hot = erow == idx
        work = jnp.where(hot, -jnp.inf, work)
        top_v.append(m)
        hots.append(hot)
        idxs.append(idx)
    sel = hots[0] | hots[1] | hots[2] | hots[3]
    selb = sel.astype(BF16)
    incl = _dot(selb, tri)
    total = _dot(selb, jnp.ones((TM, TM), BF16))
    rank_all = base + incl - 1.0
    ex = [jnp.exp(v - top_v[0]) for v in top_v]
    denom = ex[0] + ex[1] + ex[2] + ex[3]
    ri = jnp.zeros((8, TM), F32)
    rw = jnp.zeros((8, TM), F32)
    for k in range(TOP_K):
        rank_k = jnp.sum(jnp.where(hots[k], rank_all, 0.0), axis=0, keepdims=True)
        ri = jnp.where(r8 == k, idxs[k], ri)
        ri = jnp.where(r8 == TOP_K + k, rank_k, ri)
        rw = jnp.where(r8 == k, ex[k] / denom, rw)
    ri_ref[...] = ri.astype(I32)
    rw_ref[...] = rw
    base = base + total
    base_ref[...] = base
    cnt_ref[...] = base[:, 0:128]


def _merge(x, p, scan_out, tile_mod, mod, lw, n_ctx_tiles):
    n = x.shape[0]
    o_f, hl_f, o_b, hl_b = scan_out

    def tok(width):
        return pl.BlockSpec((TM, width), lambda i, tm: (i, 0))

    def pcol(c):
        return pl.BlockSpec((TM, W_MIX), lambda i, tm: (i, c))

    grid_spec = pltpu.PrefetchScalarGridSpec(
        num_scalar_prefetch=1,
        grid=(n // TM,),
        in_specs=[tok(D_MODEL), pcol(4), pcol(5), pcol(6), pcol(7), pcol(9),
                  tok(W_MIX), tok(W_MIX), tok(W_MIX), tok(W_MIX),
                  pl.BlockSpec((1, 1, N_MOD * D_MODEL), lambda i, tm: (tm[i], 0, 0)),
                  _const_spec((1, D_MODEL)),
                  _const_spec((D_MODEL, 3 * D_MODEL)), _const_spec((1, 3 * D_MODEL)),
                  _const_spec((3, W_MIX, D_MODEL)), _const_spec((D_MODEL, D_MODEL)),
                  _const_spec((1, W_MIX)), _const_spec((3, W_MIX)),
                  _const_spec((1, D_MODEL)), _const_spec((N_EXPERTS, D_MODEL)),
                  _const_spec((N_EXPERTS, 1))],
        out_specs=[tok(D_MODEL), tok(D_MODEL),
                   pl.BlockSpec((8, TM), lambda i, tm: (0, i)),
                   pl.BlockSpec((8, TM), lambda i, tm: (0, i)),
                   pl.BlockSpec((N_EXPERTS, 128), lambda i, tm: (0, 0))],
        scratch_shapes=[pltpu.VMEM((N_EXPERTS, TM), F32)],
    )
    return pl.pallas_call(
        functools.partial(_merge_kernel, n_ctx_tiles),
        grid_spec=grid_spec,
        out_shape=[jax.ShapeDtypeStruct((n, D_MODEL), F32), jax.ShapeDtypeStruct((n, D_MODEL), F32),
                   jax.ShapeDtypeStruct((8, n), I32), jax.ShapeDtypeStruct((8, n), F32),
                   jax.ShapeDtypeStruct((N_EXPERTS, 128), F32)],
        compiler_params=pltpu.CompilerParams(
            dimension_semantics=("arbitrary",), vmem_limit_bytes=VMEM_LIMIT),
        name="merge_router",
    )(tile_mod, x, p, p, p, p, p, o_f, o_b, hl_f, hl_b, mod,
      lw["norm_mix"], lw["w_merge"], lw["b_merge"], lw["w_branch"], lw["w_out"],
      lw["hgrn_norm"], lw["sconv_w"], lw["norm_ffn"], lw["w_router_t"], lw["b_router"])


def _row_copy(src, src_row, dst, dst_row, sem):
    return pltpu.make_async_copy(src.at[pl.ds(src_row, 1)], dst.at[pl.ds(dst_row, 1)], sem)


def _dispatch_kernel(off_ref, route_ref, h2_ref, xs_in_ref, xs_ref, sem):
    del xs_in_ref
    i = pl.program_id(0)

    def issue(r, carry):
        for k in range(TOP_K):
            slot = off_ref[route_ref[0, 0, k * TM + r]] + route_ref[0, 0, (TOP_K + k) * TM + r]
            _row_copy(h2_ref, i * TM + r, xs_ref, slot, sem).start()
        return carry

    lax.fori_loop(0, TM, issue, 0)

    def drain(r, carry):
        for k in range(TOP_K):
            _row_copy(h2_ref, 0, xs_ref, 0, sem).wait()
        return carry

    lax.fori_loop(0, TM, drain, 0)


def _dispatch(offsets, route, h2, xs_init):
    n = h2.shape[0]
    grid_spec = pltpu.PrefetchScalarGridSpec(
        num_scalar_prefetch=1,
        grid=(n // TM,),
        in_specs=[pl.BlockSpec((1, 1, 8 * TM), lambda i, off: (i, 0, 0), memory_space=pltpu.SMEM),
                  pl.BlockSpec(memory_space=pl.ANY),
                  pl.BlockSpec(memory_space=pl.ANY)],
        out_specs=pl.BlockSpec(memory_space=pl.ANY),
        scratch_shapes=[pltpu.SemaphoreType.DMA],
    )
    return pl.pallas_call(
        _dispatch_kernel,
        grid_spec=grid_spec,
        out_shape=jax.ShapeDtypeStruct(xs_init.shape, xs_init.dtype),
        input_output_aliases={3: 0},
        compiler_params=pltpu.CompilerParams(dimension_semantics=("arbitrary",)),
        name="moe_dispatch",
    )(offsets, route, h2, xs_init)


def _ffn_kernel(te_ref, nused_ref, xs_ref, wgu_ref, bgu_ref, wd_ref, bd_ref, ys_ref, wgu_bf, wd_bf):
    i = pl.program_id(0)
    used = i < nused_ref[0]

    @pl.when(used)
    def _compute():
        prev = te_ref[jnp.maximum(i - 1, 0)]

        @pl.when((i == 0) | (te_ref[i] != prev))
        def _cast_weights():
            wgu_bf[...] = wgu_ref[0, 0].astype(BF16)
            wd_bf[...] = wd_ref[0, 0].astype(BF16)

        gu = _dot(xs_ref[...].astype(BF16), wgu_bf[...]) + bgu_ref[0, 0]
        glu = jnp.minimum(gu[:, 0:D_FF], SWIGLU_LIMIT)
        lin = jnp.clip(gu[:, D_FF:2 * D_FF], -SWIGLU_LIMIT, SWIGLU_LIMIT)
        act = glu * _sigmoid(SWIGLU_ALPHA * glu) * (lin + 1.0)
        ys_ref[...] = _dot(act.astype(BF16), wd_bf[...]) + bd_ref[0, 0]

    @pl.when(jnp.logical_not(used))
    def _skip():
        ys_ref[...] = jnp.zeros_like(ys_ref)


def _ffn(layer, tile_expert, n_used, xs, w_gate_up, b_gate_up, w_down, b_down):
    rows = xs.shape[0]
    depth = w_gate_up.shape[0]
    grid_spec = pltpu.PrefetchScalarGridSpec(
        num_scalar_prefetch=2,
        grid=(rows // TM,),
        in_specs=[
            pl.BlockSpec((TM, D_MODEL), lambda i, te, nu: (jnp.minimum(i, nu[0] - 1), 0)),
            pl.BlockSpec((1, 1, D_MODEL, 2 * D_FF), lambda i, te, nu: (layer, te[i], 0, 0)),
            pl.BlockSpec((1, 1, 1, 2 * D_FF), lambda i, te, nu: (layer, te[i], 0, 0)),
            pl.BlockSpec((1, 1, D_FF, D_MODEL), lambda i, te, nu: (layer, te[i], 0, 0)),
            pl.BlockSpec((1, 1, 1, D_MODEL), lambda i, te, nu: (layer, te[i], 0, 0)),
        ],
        out_specs=pl.BlockSpec((TM, D_MODEL), lambda i, te, nu: (i, 0)),
        scratch_shapes=[pltpu.VMEM((D_MODEL, 2 * D_FF), BF16), pltpu.VMEM((D_FF, D_MODEL), BF16)],
    )
    return pl.pallas_call(
        _ffn_kernel,
        grid_spec=grid_spec,
        out_shape=jax.ShapeDtypeStruct((rows, D_MODEL), F32),
        compiler_params=pltpu.CompilerParams(
            dimension_semantics=("arbitrary",), vmem_limit_bytes=VMEM_LIMIT),
        name="moe_ffn",
    )(tile_expert, n_used, xs, w_gate_up, b_gate_up.reshape(depth, N_EXPERTS, 1, 2 * D_FF),
      w_down, b_down.reshape(depth, N_EXPERTS, 1, D_MODEL))


def _combine_kernel(final, off_ref, tmod_ref, route_ref, x1_ref, rw_ref, mod_ref, nfin_ref, ys_ref,
                    out_ref, buf, sem):
    del tmod_ref

    def issue(r, carry):
        for k in range(TOP_K):
            slot = off_ref[route_ref[0, 0, k * TM + r]] + route_ref[0, 0, (TOP_K + k) * TM + r]
            pltpu.make_async_copy(ys_ref.at[pl.ds(slot, 1)], buf.at[k, pl.ds(r, 1)], sem).start()
        return carry

    lax.fori_loop(0, TM, issue, 0)

    def drain(r, carry):
        for k in range(TOP_K):
            pltpu.make_async_copy(ys_ref.at[pl.ds(0, 1)], buf.at[0, pl.ds(0, 1)], sem).wait()
        return carry

    lax.fori_loop(0, TM, drain, 0)

    rw = rw_ref[...]
    acc = rw[:, 0:1] * buf[0]
    for k in range(1, TOP_K):
        acc = acc + rw[:, k:k + 1] * buf[k]
    out = x1_ref[...] + mod_ref[0][:, 5 * D_MODEL:6 * D_MODEL] * acc
    if final:
        out = _rms(out, nfin_ref[...])
    out_ref[...] = out


def _combine(offsets, tile_mod, route, x1, rw_t, mod, norm_final, ys, final):
    n = x1.shape[0]
    grid_spec = pltpu.PrefetchScalarGridSpec(
        num_scalar_prefetch=2,
        grid=(n // TM,),
        in_specs=[pl.BlockSpec((1, 1, 8 * TM), lambda i, off, tm: (i, 0, 0), memory_space=pltpu.SMEM),
                  pl.BlockSpec((TM, D_MODEL), lambda i, off, tm: (i, 0)),
                  pl.BlockSpec((TM, 8), lambda i, off, tm: (i, 0)),
                  pl.BlockSpec((1, 1, N_MOD * D_MODEL), lambda i, off, tm: (tm[i], 0, 0)),
                  pl.BlockSpec((1, D_MODEL), lambda i, off, tm: (0, 0)),
                  pl.BlockSpec(memory_space=pl.ANY)],
        out_specs=pl.BlockSpec((TM, D_MODEL), lambda i, off, tm: (i, 0)),
        scratch_shapes=[pltpu.VMEM((TOP_K, TM, D_MODEL), F32), pltpu.SemaphoreType.DMA],
    )
    return pl.pallas_call(
        functools.partial(_combine_kernel, final),
        grid_spec=grid_spec,
        out_shape=jax.ShapeDtypeStruct((n, D_MODEL), F32),
        compiler_params=pltpu.CompilerParams(
            dimension_semantics=("arbitrary",), vmem_limit_bytes=VMEM_LIMIT),
        name="moe_combine",
    )(offsets, tile_mod, route, x1, rw_t, mod, norm_final, ys)


def _moe(layer, h2, ri, rw, counts, x1, tile_mod, mod, norm_final, lw, final):
    n = h2.shape[0]
    n_tiles = n // TM
    slots = n * TOP_K + N_EXPERTS * TM
    cnt = counts[:, 0].astype(I32)
    padded = ((cnt + TM - 1) // TM) * TM
    ends = jnp.cumsum(padded)
    offsets = (ends - padded).astype(I32)
    n_used = (ends[-1] // TM).astype(I32)
    tile_ids = jnp.minimum(jnp.arange(slots // TM, dtype=I32), n_used - 1)
    tile_expert = jnp.searchsorted(ends // TM, tile_ids, side="right").astype(I32)
    route = ri.reshape(8, n_tiles, TM).transpose(1, 0, 2).reshape(n_tiles, 1, 8 * TM)

    xs = _dispatch(offsets, route, h2, jnp.zeros((slots, D_MODEL), F32))
    ys = _ffn(layer, tile_expert, n_used.reshape(1), xs,
              lw["w_gate_up"], lw["b_gate_up"], lw["w_down"], lw["b_down"])
    return _combine(offsets, tile_mod, route, x1, rw.T, mod, norm_final, ys, final)


def _block_diag_pairs(w):
    z = jnp.zeros_like(w[:, 0::2])
    top = jnp.concatenate([w[:, 0::2], z], axis=-1)
    bot = jnp.concatenate([z, w[:, 1::2]], axis=-1)
    return jnp.concatenate([top, bot], axis=-2)


def kernel(x_prompt, x_sample, state_hgrn, state_rglru, c, c_ctx, norm_mix_w, norm_ffn_w, norm_final_w, w_mod, b_mod, w_in, hgrn_lb_logits, hgrn_norm_w, sconv_w, lru_conv_w, lru_conv_b, lru_wa, lru_ba, lru_wi, lru_bi, lru_lambda, w_branch, w_merge, b_merge, w_out, w_router, b_router, w_gate_up, b_gate_up, w_down, b_down):
    depth = w_in.shape[0]
    bp, tp, _ = x_prompt.shape
    bs, ts, _ = x_sample.shape
    assert tp == TM and ts % TM == 0 and bs + 1 <= MOD_ROWS
    n_ctx = bp * tp
    n_ctx_tiles = n_ctx // TM
    nt_lat = ts // TM

    x = jnp.concatenate([x_prompt.reshape(n_ctx, D_MODEL), x_sample.reshape(bs * ts, D_MODEL)], axis=0)
    n_tiles = x.shape[0] // TM

    lb_cum = jnp.cumsum(jax.nn.softmax(hgrn_lb_logits.astype(F32), axis=0), axis=0)
    lower_bounds = lb_cum - lb_cum[0:1]

    cc = jnp.zeros((MOD_ROWS, D_MODEL), F32).at[0].set(c_ctx).at[1:1 + bs].set(c)
    mod_all = _modulation(cc, w_mod, b_mod)
    tile_mod = jnp.concatenate([jnp.zeros((n_ctx_tiles,), I32),
                                1 + jnp.arange(bs * nt_lat, dtype=I32) // nt_lat])

    lat = jnp.arange(bs * nt_lat, dtype=I32)
    lat_b, lat_c = lat // nt_lat, lat % nt_lat
    ctx = jnp.arange(bp, dtype=I32)
    one = jnp.ones_like(ctx)
    tabs = (
        jnp.concatenate([ctx, n_ctx_tiles + lat]),
        jnp.concatenate([ctx, n_ctx_tiles + lat_b * nt_lat + (nt_lat - 1 - lat_c)]),
        jnp.concatenate([one, (lat_c == 0).astype(I32)]),
        jnp.concatenate([one, (lat_c == nt_lat - 1).astype(I32)]),
        jnp.concatenate([ctx, bp + lat_b]),
        jnp.concatenate([one * (TM - 1), jnp.full_like(lat, GRID_W - 1)]),
    )

    new_hgrn, new_lru = [], []
    for l in range(depth):
        lw = dict(
            norm_mix=norm_mix_w[l].reshape(1, D_MODEL), norm_ffn=norm_ffn_w[l].reshape(1, D_MODEL),
            w_merge=w_merge[l].astype(BF16), b_merge=b_merge[l].reshape(1, 3 * D_MODEL),
            w_branch=w_branch[l].astype(BF16), w_out=w_out[l].astype(BF16),
            hgrn_norm=hgrn_norm_w[l].reshape(1, W_MIX), sconv_w=sconv_w[l],
            w_router_t=w_router[l].T, b_router=b_router[l].reshape(N_EXPERTS, 1),
            w_gate_up=w_gate_up, b_gate_up=b_gate_up, w_down=w_down, b_down=b_down)
        mod = mod_all[l].reshape(MOD_ROWS, 1, N_MOD * D_MODEL)

        p = _premix(x, tile_mod, mod, lw["norm_mix"], w_in[l].astype(BF16))

        lru_par = jnp.concatenate(
            [lru_conv_w[l], lru_conv_b[l][None], lru_ba[l], lru_bi[l], lru_lambda[l],
             jnp.zeros((5, W_MIX), F32)], axis=0)
        w_lru = jnp.concatenate([_block_diag_pairs(lru_wa[l]), _block_diag_pairs(lru_wi[l])], axis=-1).astype(BF16)
        s0_hgrn = jnp.concatenate([jnp.zeros((bp,) + state_hgrn.shape[2:], F32), state_hgrn[:, l]], axis=0)
        s0_lru = jnp.concatenate([jnp.zeros((bp, 2, W_MIX), F32), state_rglru[:, l]], axis=0)
        o_f, hl_f, o_b, hl_b, s_hgrn, s_lru = _scans(p, tabs, lower_bounds[l], lru_par, w_lru, s0_hgrn, s0_lru)
        new_hgrn.append(s_hgrn[:bp])
        new_lru.append(s_lru[:bp])

        x1, h2, ri, rw, counts = _merge(x, p, (o_f, hl_f, o_b, hl_b), tile_mod, mod, lw, n_ctx_tiles)
        x = _moe(l, h2, ri, rw, counts, x1, tile_mod, mod, norm_final_w.reshape(1, D_MODEL), lw,
                 final=(l == depth - 1))

    y_prompt = x[:n_ctx].reshape(bp, tp, D_MODEL)
    y_sample = x[n_ctx:].reshape(bs, ts, D_MODEL)
    return (y_prompt, y_sample, jnp.stack(new_hgrn, axis=1), jnp.stack(new_lru, axis=1))
```

```python
import functools

import jax
import jax.numpy as jnp
from jax import lax
from jax.experimental import pallas as pl
from jax.experimental.pallas import tpu as pltpu

F32 = jnp.float32
BF16 = jnp.bfloat16
I32 = jnp.int32

D_MODEL = 1024
W_MIX = 512
N_HEADS = 4
D_HEAD = 128
N_IN = 10
N_MOD = 6
EPS = 1e-6
LRU_C = 8.0
N_EXPERTS = 32
TOP_K = 4
D_FF = 1024
SWIGLU_LIMIT = 7.0
SWIGLU_ALPHA = 1.702

TM = 256
CHUNK = 64
SUB = 16
N_SUB = CHUNK // SUB
SUB_SHIFT = SUB.bit_length() - 1
N_CHUNK = TM // CHUNK
GRID_W = 64
MOD_ROWS = 8
VMEM_LIMIT = 56 * 1024 * 1024


def _dot(a, b):
    return jnp.dot(a, b, preferred_element_type=F32)


def _dot_nt(a, b):
    return lax.dot_general(a, b, (((1,), (1,)), ((), ())), preferred_element_type=F32)


def _sigmoid(x):
    return 1.0 / (1.0 + jnp.exp(-x))


def _rms(x, w):
    return x * lax.rsqrt(jnp.mean(x * x, axis=-1, keepdims=True) + EPS) * w


def _split3(x):
    a = x.astype(BF16)
    r = x - a.astype(F32)
    b = r.astype(BF16)
    c = (r - b.astype(F32)).astype(BF16)
    return a, b, c


def _const_spec(shape):
    n = len(shape)
    return pl.BlockSpec(shape, lambda *_: (0,) * n, pipeline_mode=pl.Buffered(1))


def _mod_kernel(c_ref, w_ref, b_ref, o_ref):
    c = c_ref[...]
    s = (c * _sigmoid(c)).astype(BF16)
    o_ref[0] = _dot(s, w_ref[0].astype(BF16)) + b_ref[0]


def _modulation(cc, w_mod, b_mod):
    depth, _, width = w_mod.shape
    nb = 1536
    return pl.pallas_call(
        _mod_kernel,
        grid=(depth, width // nb),
        in_specs=[
            pl.BlockSpec((MOD_ROWS, D_MODEL), lambda l, j: (0, 0)),
            pl.BlockSpec((1, D_MODEL, nb), lambda l, j: (l, 0, j)),
            pl.BlockSpec((1, 1, nb), lambda l, j: (l, 0, j)),
        ],
        out_specs=pl.BlockSpec((1, MOD_ROWS, nb), lambda l, j: (l, 0, j)),
        out_shape=jax.ShapeDtypeStruct((depth, MOD_ROWS, width), F32),
        compiler_params=pltpu.CompilerParams(
            dimension_semantics=("arbitrary", "arbitrary"), vmem_limit_bytes=VMEM_LIMIT),
        name="modulation",
    )(cc, w_mod, b_mod.reshape(depth, 1, width))


def _premix_kernel(tmod_ref, x_ref, mod_ref, nw_ref, win_ref, p_ref):
    del tmod_ref
    mod = mod_ref[0]
    h = _rms(x_ref[...], nw_ref[...]) * (1.0 + mod[:, D_MODEL:2 * D_MODEL]) + mod[:, 0:D_MODEL]
    p_ref[...] = _dot(h.astype(BF16), win_ref[...])


def _premix(x, tile_mod, mod, norm_w, w_in_bf):
    n = x.shape[0]
    width = w_in_bf.shape[1]
    grid_spec = pltpu.PrefetchScalarGridSpec(
        num_scalar_prefetch=1,
        grid=(n // TM,),
        in_specs=[
            pl.BlockSpec((TM, D_MODEL), lambda i, tm: (i, 0)),
            pl.BlockSpec((1, 1, N_MOD * D_MODEL), lambda i, tm: (tm[i], 0, 0)),
            _const_spec((1, D_MODEL)),
            _const_spec((D_MODEL, width)),
        ],
        out_specs=pl.BlockSpec((TM, width), lambda i, tm: (i, 0)),
    )
    return pl.pallas_call(
        _premix_kernel,
        grid_spec=grid_spec,
        out_shape=jax.ShapeDtypeStruct((n, width), F32),
        compiler_params=pltpu.CompilerParams(
            dimension_semantics=("arbitrary",), vmem_limit_bytes=VMEM_LIMIT),
        name="premix",
    )(tile_mod, x, mod, norm_w, w_in_bf)


def _shift_rows(x, s):
    return pltpu.roll(x, s % x.shape[0], axis=0)


def _lru_scan(a, b, reverse):
    n = a.shape[0]
    row = lax.broadcasted_iota(I32, a.shape, 0)
    s = 1
    while s < n:
        if reverse:
            keep = row < n - s
            a_sh = jnp.where(keep, _shift_rows(a, -s), 1.0)
            b_sh = jnp.where(keep, _shift_rows(b, -s), 0.0)
        else:
            keep = row >= s
            a_sh = jnp.where(keep, _shift_rows(a, s), 1.0)
            b_sh = jnp.where(keep, _shift_rows(b, s), 0.0)
        b = a * b_sh + b
        a = a * a_sh
        s *= 2
    return a, b


def _scan_kernel(tf_ref, tb_ref, first_ref, last_ref, sb_ref, lm1_ref,
                 qf_ref, ff_ref, vf_ref, rxf_ref, qb_ref, fb_ref, vb_ref, rxb_ref,
                 lb_ref, lp_ref, wlru_ref, s0h_ref, s0l_ref,
                 of_ref, hlf_ref, ob_ref, hlb_ref, sfh_ref, sfl_ref,
                 st_ref, hcar_ref, qs_ref, ks_ref, gs_ref, lcs_ref, kcs_ref):
    del tf_ref, tb_ref, sb_ref
    step = pl.program_id(0)

    @pl.when(first_ref[step] == 1)
    def _init():
        for d in range(2):
            for h in range(N_HEADS):
                st_ref[d, h] = s0h_ref[0, d, h].T
        hcar_ref[0:2, :] = s0l_ref[0]

    lm1 = lm1_ref[step]
    row = lax.broadcasted_iota(I32, (TM, W_MIX), 0)
    pos = row & lm1

    dirs = ((qf_ref, ff_ref, rxf_ref, hlf_ref), (qb_ref, fb_ref, rxb_ref, hlb_ref))
    for d, (q_ref, f_ref, rx_ref, hl_ref) in enumerate(dirs):
        q = q_ref[...]
        qs_ref[d] = q * _sigmoid(q) * (D_HEAD ** -0.5)
        f = f_ref[...]
        e = jnp.exp(-jnp.abs(f))
        r = 1.0 / (1.0 + e)
        sig_neg = jnp.where(f >= 0, e * r, r)
        log_sig = jnp.minimum(f, 0.0) - jnp.log1p(e)
        lb = lb_ref[d:d + 1, :]
        ks_ref[d] = (1.0 - lb) * sig_neg
        la = jnp.log(lb)
        lbb = jnp.log1p(-lb) + log_sig
        gs_ref[d] = jnp.maximum(la, lbb) + jnp.log1p(jnp.exp(-jnp.abs(la - lbb)))

        u = rx_ref[...]
        xr = (lp_ref[2:3, :] * u + lp_ref[4:5, :]
              + lp_ref[0:1, :] * jnp.where(pos >= 2, _shift_rows(u, 2), 0.0)
              + lp_ref[1:2, :] * jnp.where(pos >= 1, _shift_rows(u, 1), 0.0)
              + lp_ref[3:4, :] * jnp.where(pos < lm1, _shift_rows(u, -1), 0.0))
        rl, il = [], []
        for nblk in range(W_MIX // 128):
            z = _dot(xr[:, 128 * nblk:128 * (nblk + 1)].astype(BF16), wlru_ref[d, nblk])
            rl.append(z[:, :128])
            il.append(z[:, 128:])
        r_gate = _sigmoid(jnp.concatenate(rl, axis=1) + lp_ref[5 + d:6 + d, :])
        i_gate = _sigmoid(jnp.concatenate(il, axis=1) + lp_ref[7 + d:8 + d, :])
        lam = lp_ref[9 + d:10 + d, :]
        softplus = jnp.maximum(-lam, 0.0) + jnp.log1p(jnp.exp(-jnp.abs(lam)))
        log_a = (-LRU_C) * r_gate * softplus
        a = jnp.exp(log_a)
        th = jnp.tanh(log_a)
        bx = jnp.sqrt(-2.0 * th / (1.0 - th)) * i_gate * xr
        a_cum, b_cum = _lru_scan(a, bx, reverse=(d == 1))
        hd = a_cum * hcar_ref[d:d + 1, :] + b_cum
        hl_ref[...] = hd
        edge = TM - 1 if d == 0 else 0
        hcar_ref[d:d + 1, :] = hd[edge:edge + 1, :]

    r64 = lax.broadcasted_iota(I32, (CHUNK, CHUNK), 0)
    c64 = lax.broadcasted_iota(I32, (CHUNK, CHUNK), 1)
    same_sub = (r64 >> SUB_SHIFT) == (c64 >> SUB_SHIFT)
    rel = c64 - ((r64 >> SUB_SHIFT) << SUB_SHIFT)
    rmod = r64 & (SUB - 1)
    sub_of_row = lax.broadcasted_iota(I32, (CHUNK, W_MIX), 0) >> SUB_SHIFT
    ones_red = jnp.ones((D_HEAD, CHUNK), BF16)
    v_refs = (vf_ref, vb_ref)
    o_refs = (of_ref, ob_ref)

    def chunk_body(ci, carry):
        for d in range(2):
            rev = d == 1
            c0 = pl.multiple_of((N_CHUNK - 1 - ci) * CHUNK if rev else ci * CHUNK, CHUNK)
            order = tuple(reversed(range(N_SUB))) if rev else tuple(range(N_SUB))
            q = qs_ref[d, pl.ds(c0, CHUNK), :]
            k = ks_ref[d, pl.ds(c0, CHUNK), :]
            g = gs_ref[d, pl.ds(c0, CHUNK), :]
            v = v_refs[d][pl.ds(c0, CHUNK), :]

            tri = same_sub & ((c64 >= r64) if rev else (c64 <= r64))
            sel = jnp.concatenate([tri.astype(BF16), same_sub.astype(BF16)], axis=0)
            res = _dot(sel, jnp.concatenate(_split3(g), axis=1))
            res = res[:, 0:W_MIX] + res[:, W_MIX:2 * W_MIX] + res[:, 2 * W_MIX:3 * W_MIX]
            lc = res[0:CHUNK]
            totb = res[CHUNK:2 * CHUNK]
            lcs_ref[d] = lc
            kcs_ref[d] = k

            qh = q * jnp.exp(lc)
            kt = k * jnp.exp(totb - lc)

            tot = [totb[SUB * a:SUB * a + 1, :] for a in order]
            cum = [tot[0]]
            for i in range(1, N_SUB):
                cum.append(cum[-1] + tot[i])
            masks = [sub_of_row == a for a in order]

            def rows_select(vals):
                out = vals[-1]
                for m, val in zip(reversed(masks[:-1]), reversed(vals[:-1])):
                    out = jnp.where(m, val, out)
                return out

            one = jnp.ones((1, W_MIX), F32)
            zero = jnp.zeros((1, W_MIX), F32)
            qe = qh * rows_select([one] + [jnp.exp(cum[i - 1]) for i in range(1, N_SUB)])
            kl = kt * rows_select([jnp.exp(cum[-1] - cum[j]) for j in range(N_SUB - 1)] + [one])
            chunk_decay = jnp.exp(cum[-1])
            lhs, kts = [], []
            for j in range(N_SUB - 1):
                fac = [zero] * (j + 1) + [one] + [jnp.exp(cum[i - 1] - cum[j]) for i in range(j + 2, N_SUB)]
                lhs.append((qh * rows_select(fac)).astype(BF16))
                kts.append(jnp.where(masks[j], kt, 0.0).astype(BF16))

            att_d = [jnp.zeros((CHUNK, CHUNK), F32) for _ in range(N_HEADS)]
            for s in range(SUB):
                ksel = jnp.concatenate(
                    [jnp.broadcast_to(kcs_ref[d, SUB * a + s:SUB * a + s + 1, :], (SUB, W_MIX))
                     for a in range(N_SUB)], axis=0)
                lsel = jnp.concatenate(
                    [jnp.broadcast_to(lcs_ref[d, SUB * a + s:SUB * a + s + 1, :], (SUB, W_MIX))
                     for a in range(N_SUB)], axis=0)
                z = (q * ksel * jnp.exp(jnp.minimum(lc - lsel, 0.0))).astype(BF16)
                dmask = (rel == s) & ((rmod <= s) if rev else (rmod >= s))
                for h in range(N_HEADS):
                    col = _dot(z[:, D_HEAD * h:D_HEAD * (h + 1)], ones_red)
                    att_d[h] = jnp.where(dmask, col, att_d[h])

            outs = []
            for h in range(N_HEADS):
                sl = slice(D_HEAD * h, D_HEAD * (h + 1))
                att = att_d[h]
                for j in range(N_SUB - 1):
                    att = att + _dot_nt(lhs[j][:, sl], kts[j][:, sl])
                st = st_ref[d, h]
                vh = v[:, sl]
                o = _dot_nt(qe[:, sl].astype(BF16), st.astype(BF16)) + _dot(att.astype(BF16), vh.astype(BF16))
                outs.append(o)
                st_ref[d, h] = st * chunk_decay[:, sl] + _dot(vh.T.astype(BF16), kl[:, sl].astype(BF16))
            o_refs[d][pl.ds(c0, CHUNK), :] = jnp.concatenate(outs, axis=1)
        return carry

    lax.fori_loop(0, N_CHUNK, chunk_body, 0)

    @pl.when(last_ref[step] == 1)
    def _final():
        for d in range(2):
            for h in range(N_HEADS):
                sfh_ref[0, d, h] = st_ref[d, h].T
        sfl_ref[0] = hcar_ref[0:2, :]


def _scans(p, tabs, lb, lru_par, w_lru, s0_hgrn, s0_lru):
    n = p.shape[0]
    n_steps = tabs[0].shape[0]
    n_seq = s0_hgrn.shape[0]

    def col(c, which):
        if which == 0:
            return pl.BlockSpec((TM, W_MIX), lambda s, tf, tb, fi, la, sb, lm: (tf[s], c))
        return pl.BlockSpec((TM, W_MIX), lambda s, tf, tb, fi, la, sb, lm: (tb[s], c))

    def out_col(which):
        if which == 0:
            return pl.BlockSpec((TM, W_MIX), lambda s, tf, tb, fi, la, sb, lm: (tf[s], 0))
        return pl.BlockSpec((TM, W_MIX), lambda s, tf, tb, fi, la, sb, lm: (tb[s], 0))

    st_spec = pl.BlockSpec((1, 2, N_HEADS, D_HEAD, D_HEAD),
                           lambda s, tf, tb, fi, la, sb, lm: (sb[s], 0, 0, 0, 0))
    sl_spec = pl.BlockSpec((1, 2, W_MIX), lambda s, tf, tb, fi, la, sb, lm: (sb[s], 0, 0))
    grid_spec = pltpu.PrefetchScalarGridSpec(
        num_scalar_prefetch=6,
        grid=(n_steps,),
        in_specs=[col(0, 0), col(1, 0), col(3, 0), col(8, 0),
                  col(0, 1), col(2, 1), col(3, 1), col(8, 1),
                  _const_spec((2, W_MIX)), _const_spec((16, W_MIX)),
                  _const_spec((2, W_MIX // 128, 128, 256)),
                  st_spec, sl_spec],
        out_specs=[out_col(0), out_col(0), out_col(1), out_col(1), st_spec, sl_spec],
        scratch_shapes=[
            pltpu.VMEM((2, N_HEADS, D_HEAD, D_HEAD), F32),
            pltpu.VMEM((8, W_MIX), F32),
            pltpu.VMEM((2, TM, W_MIX), F32),
            pltpu.VMEM((2, TM, W_MIX), F32),
            pltpu.VMEM((2, TM, W_MIX), F32),
            pltpu.VMEM((2, CHUNK, W_MIX), F32),
            pltpu.VMEM((2, CHUNK, W_MIX), F32),
        ],
    )
    row = jax.ShapeDtypeStruct((n, W_MIX), F32)
    return pl.pallas_call(
        _scan_kernel,
        grid_spec=grid_spec,
        out_shape=[row, row, row, row,
                   jax.ShapeDtypeStruct((n_seq, 2, N_HEADS, D_HEAD, D_HEAD), F32),
                   jax.ShapeDtypeStruct((n_seq, 2, W_MIX), F32)],
        compiler_params=pltpu.CompilerParams(
            dimension_semantics=("arbitrary",), vmem_limit_bytes=VMEM_LIMIT),
        name="scans",
    )(*tabs, p, p, p, p, p, p, p, p, lb, lru_par, w_lru, s0_hgrn, s0_lru)


def _merge_kernel(n_ctx_tiles, tmod_ref,
                  x_ref, go_ref, sb_ref, sc_ref, sx_ref, rg_ref, of_ref, ob_ref, hlf_ref, hlb_ref,
                  mod_ref, nmix_ref, wmerge_ref, bmerge_ref, wbr_ref, wout_ref, hnorm_ref, scw_ref,
                  nffn_ref, wrt_ref, br_ref,
                  x1_ref, h2_ref, ri_ref, rw_ref, cnt_ref, base_ref):
    del tmod_ref
    i = pl.program_id(0)

    @pl.when(i == 0)
    def _init():
        base_ref[...] = jnp.zeros_like(base_ref)

    mod = mod_ref[0]
    x = x_ref[...]
    h = _rms(x, nmix_ref[...]) * (1.0 + mod[:, D_MODEL:2 * D_MODEL]) + mod[:, 0:D_MODEL]
    gates = _sigmoid(_dot(h.astype(BF16), wmerge_ref[...]) + bmerge_ref[...])

    o = of_ref[...] + ob_ref[...]
    go = go_ref[...]
    parts = []
    for hh in range(N_HEADS):
        sl = slice(D_HEAD * hh, D_HEAD * (hh + 1))
        parts.append(_rms(o[:, sl], hnorm_ref[:, sl]))
    ya = jnp.concatenate(parts, axis=1) * (go * _sigmoid(go))

    lm1 = jnp.where(i < n_ctx_tiles, TM - 1, GRID_W - 1)
    pos = lax.broadcasted_iota(I32, (TM, W_MIX), 0) & lm1
    u = sc_ref[...] * sx_ref[...]
    conv = (scw_ref[1:2, :] * u
            + scw_ref[0:1, :] * jnp.where(pos >= 1, _shift_rows(u, 1), 0.0)
            + scw_ref[2:3, :] * jnp.where(pos < lm1, _shift_rows(u, -1), 0.0))
    yb = sb_ref[...] * conv

    yc = (hlf_ref[...] + hlb_ref[...]) * jax.nn.gelu(rg_ref[...])

    merged = (gates[:, 0:D_MODEL] * _dot(ya.astype(BF16), wbr_ref[0])
              + gates[:, D_MODEL:2 * D_MODEL] * _dot(yb.astype(BF16), wbr_ref[1])
              + gates[:, 2 * D_MODEL:3 * D_MODEL] * _dot(yc.astype(BF16), wbr_ref[2]))
    y = _dot(merged.astype(BF16), wout_ref[...])
    x1 = x + mod[:, 2 * D_MODEL:3 * D_MODEL] * y
    x1_ref[...] = x1

    h2 = _rms(x1, nffn_ref[...]) * (1.0 + mod[:, 4 * D_MODEL:5 * D_MODEL]) + mod[:, 3 * D_MODEL:4 * D_MODEL]
    h2_ref[...] = h2

    h_hi = h2.astype(BF16)
    h_lo = (h2 - h_hi.astype(F32)).astype(BF16)
    wr = wrt_ref[...]
    w_hi = wr.astype(BF16)
    w_lo = (wr - w_hi.astype(F32)).astype(BF16)
    logits = _dot_nt(w_hi, h_hi) + _dot_nt(w_hi, h_lo) + _dot_nt(w_lo, h_hi) + br_ref[...]

    erow = lax.broadcasted_iota(I32, (N_EXPERTS, TM), 0).astype(F32)
    r8 = lax.broadcasted_iota(I32, (8, TM), 0)
    tri = (lax.broadcasted_iota(I32, (TM, TM), 0) <= lax.broadcasted_iota(I32, (TM, TM), 1)).astype(BF16)
    base = base_ref[...]
    top_v, hots, idxs = [], [], []
    work = logits
    for _ in range(TOP_K):
        m = jnp.max(work, axis=0, keepdims=True)
        idx = jnp.min(jnp.where(work == m, erow, float(N_EXPERTS)), axis=0, keepdims=True)
        hot = erow == idx
        work = jnp.where(hot, -jnp.inf, work)
        top_v.append(m)
        hots.append(hot)
        idxs.append(idx)
    sel = hots[0] | hots[1] | hots[2] | hots[3]
    selb = sel.astype(BF16)
    incl = _dot(selb, tri)
    total = _dot(selb, jnp.ones((TM, TM), BF16))
    rank_all = base + incl - 1.0
    ex = [jnp.exp(v - top_v[0]) for v in top_v]
    denom = ex[0] + ex[1] + ex[2] + ex[3]
    ri = jnp.zeros((8, TM), F32)
    rw = jnp.zeros((8, TM), F32)
    for k in range(TOP_K):
        rank_k = jnp.sum(jnp.where(hots[k], rank_all, 0.0), axis=0, keepdims=True)
        ri = jnp.where(r8 == k, idxs[k], ri)
        ri = jnp.where(r8 == TOP_K + k, rank_k, ri)
        rw = jnp.where(r8 == k, ex[k] / denom, rw)
    ri_ref[...] = ri.astype(I32)
    rw_ref[...] = rw
    base = base + total
    base_ref[...] = base
    cnt_ref[...] = base[:, 0:128]


def _merge(x, p, scan_out, tile_mod, mod, lw, n_ctx_tiles):
    n = x.shape[0]
    o_f, hl_f, o_b, hl_b = scan_out

    def tok(width):
        return pl.BlockSpec((TM, width), lambda i, tm: (i, 0))

    def pcol(c):
        return pl.BlockSpec((TM, W_MIX), lambda i, tm: (i, c))

    grid_spec = pltpu.PrefetchScalarGridSpec(
        num_scalar_prefetch=1,
        grid=(n // TM,),
        in_specs=[tok(D_MODEL), pcol(4), pcol(5), pcol(6), pcol(7), pcol(9),
                  tok(W_MIX), tok(W_MIX), tok(W_MIX), tok(W_MIX),
                  pl.BlockSpec((1, 1, N_MOD * D_MODEL), lambda i, tm: (tm[i], 0, 0)),
                  _const_spec((1, D_MODEL)),
                  _const_spec((D_MODEL, 3 * D_MODEL)), _const_spec((1, 3 * D_MODEL)),
                  _const_spec((3, W_MIX, D_MODEL)), _const_spec((D_MODEL, D_MODEL)),
                  _const_spec((1, W_MIX)), _const_spec((3, W_MIX)),
                  _const_spec((1, D_MODEL)), _const_spec((N_EXPERTS, D_MODEL)),
                  _const_spec((N_EXPERTS, 1))],
        out_specs=[tok(D_MODEL), tok(D_MODEL),
                   pl.BlockSpec((8, TM), lambda i, tm: (0, i)),
                   pl.BlockSpec((8, TM), lambda i, tm: (0, i)),
                   pl.BlockSpec((N_EXPERTS, 128), lambda i, tm: (0, 0))],
        scratch_shapes=[pltpu.VMEM((N_EXPERTS, TM), F32)],
    )
    return pl.pallas_call(
        functools.partial(_merge_kernel, n_ctx_tiles),
        grid_spec=grid_spec,
        out_shape=[jax.ShapeDtypeStruct((n, D_MODEL), F32), jax.ShapeDtypeStruct((n, D_MODEL), F32),
                   jax.ShapeDtypeStruct((8, n), I32), jax.ShapeDtypeStruct((8, n), F32),
                   jax.ShapeDtypeStruct((N_EXPERTS, 128), F32)],
        compiler_params=pltpu.CompilerParams(
            dimension_semantics=("arbitrary",), vmem_limit_bytes=VMEM_LIMIT),
        name="merge_router",
    )(tile_mod, x, p, p, p, p, p, o_f, o_b, hl_f, hl_b, mod,
      lw["norm_mix"], lw["w_merge"], lw["b_merge"], lw["w_branch"], lw["w_out"],
      lw["hgrn_norm"], lw["sconv_w"], lw["norm_ffn"], lw["w_router_t"], lw["b_router"])


def _row_copy(src, src_row, dst, dst_row, sem):
    return pltpu.make_async_copy(src.at[pl.ds(src_row, 1)], dst.at[pl.ds(dst_row, 1)], sem)


def _dispatch_kernel(off_ref, route_ref, h2_ref, xs_in_ref, xs_ref, sem):
    del xs_in_ref

    def issue(r, carry):
        for k in range(TOP_K):
            slot = off_ref[route_ref[0, 0, k * TM + r]] + route_ref[0, 0, (TOP_K + k) * TM + r]
            _row_copy(h2_ref, r, xs_ref, slot, sem).start()
        return carry

    lax.fori_loop(0, TM, issue, 0)

    def drain(r, carry):
        for k in range(TOP_K):
            _row_copy(h2_ref, 0, xs_ref, 0, sem).wait()
        return carry

    lax.fori_loop(0, TM, drain, 0)


def _dispatch(offsets, route, h2, xs_init):
    n = h2.shape[0]
    grid_spec = pltpu.PrefetchScalarGridSpec(
        num_scalar_prefetch=1,
        grid=(n // TM,),
        in_specs=[pl.BlockSpec((1, 1, 8 * TM), lambda i, off: (i, 0, 0), memory_space=pltpu.SMEM),
                  pl.BlockSpec((TM, D_MODEL), lambda i, off: (i, 0)),
                  pl.BlockSpec(memory_space=pl.ANY)],
        out_specs=pl.BlockSpec(memory_space=pl.ANY),
        scratch_shapes=[pltpu.SemaphoreType.DMA],
    )
    return pl.pallas_call(
        _dispatch_kernel,
        grid_spec=grid_spec,
        out_shape=jax.ShapeDtypeStruct(xs_init.shape, xs_init.dtype),
        input_output_aliases={3: 0},
        compiler_params=pltpu.CompilerParams(dimension_semantics=("arbitrary",)),
        name="moe_dispatch",
    )(offsets, route, h2, xs_init)


def _ffn_kernel(te_ref, nused_ref, xs_ref, wgu_ref, bgu_ref, wd_ref, bd_ref, ys_ref, wgu_bf, wd_bf):
    i = pl.program_id(0)
    used = i < nused_ref[0]

    @pl.when(used)
    def _compute():
        prev = te_ref[jnp.maximum(i - 1, 0)]

        @pl.when((i == 0) | (te_ref[i] != prev))
        def _cast_weights():
            wgu_bf[...] = wgu_ref[0, 0].astype(BF16)
            wd_bf[...] = wd_ref[0, 0].astype(BF16)

        gu = _dot(xs_ref[...].astype(BF16), wgu_bf[...]) + bgu_ref[0, 0]
        glu = jnp.minimum(gu[:, 0:D_FF], SWIGLU_LIMIT)
        lin = jnp.clip(gu[:, D_FF:2 * D_FF], -SWIGLU_LIMIT, SWIGLU_LIMIT)
        act = glu * _sigmoid(SWIGLU_ALPHA * glu) * (lin + 1.0)
        ys_ref[...] = _dot(act.astype(BF16), wd_bf[...]) + bd_ref[0, 0]

    @pl.when(jnp.logical_not(used))
    def _skip():
        ys_ref[...] = jnp.zeros_like(ys_ref)


def _ffn(layer, tile_expert, n_used, xs, w_gate_up, b_gate_up, w_down, b_down):
    rows = xs.shape[0]
    depth = w_gate_up.shape[0]
    grid_spec = pltpu.PrefetchScalarGridSpec(
        num_scalar_prefetch=2,
        grid=(rows // TM,),
        in_specs=[
            pl.BlockSpec((TM, D_MODEL), lambda i, te, nu: (jnp.minimum(i, nu[0] - 1), 0)),
            pl.BlockSpec((1, 1, D_MODEL, 2 * D_FF), lambda i, te, nu: (layer, te[i], 0, 0)),
            pl.BlockSpec((1, 1, 1, 2 * D_FF), lambda i, te, nu: (layer, te[i], 0, 0)),
            pl.BlockSpec((1, 1, D_FF, D_MODEL), lambda i, te, nu: (layer, te[i], 0, 0)),
            pl.BlockSpec((1, 1, 1, D_MODEL), lambda i, te, nu: (layer, te[i], 0, 0)),
        ],
        out_specs=pl.BlockSpec((TM, D_MODEL), lambda i, te, nu: (i, 0)),
        scratch_shapes=[pltpu.VMEM((D_MODEL, 2 * D_FF), BF16), pltpu.VMEM((D_FF, D_MODEL), BF16)],
    )
    return pl.pallas_call(
        _ffn_kernel,
        grid_spec=grid_spec,
        out_shape=jax.ShapeDtypeStruct((rows, D_MODEL), F32),
        compiler_params=pltpu.CompilerParams(
            dimension_semantics=("arbitrary",), vmem_limit_bytes=VMEM_LIMIT),
        name="moe_ffn",
    )(tile_expert, n_used, xs, w_gate_up, b_gate_up.reshape(depth, N_EXPERTS, 1, 2 * D_FF),
      w_down, b_down.reshape(depth, N_EXPERTS, 1, D_MODEL))


def _combine_kernel(final, off_ref, tmod_ref, route_ref, x1_ref, rw_ref, mod_ref, nfin_ref, ys_ref,
                    out_ref, buf, sem):
    del tmod_ref

    def issue(r, carry):
        for k in range(TOP_K):
            slot = off_ref[route_ref[0, 0, k * TM + r]] + route_ref[0, 0, (TOP_K + k) * TM + r]
            pltpu.make_async_copy(ys_ref.at[pl.ds(slot, 1)], buf.at[k, pl.ds(r, 1)], sem).start()
        return carry

    lax.fori_loop(0, TM, issue, 0)

    def drain(r, carry):
        for k in range(TOP_K):
            pltpu.make_async_copy(ys_ref.at[pl.ds(0, 1)], buf.at[0, pl.ds(0, 1)], sem).wait()
        return carry

    lax.fori_loop(0, TM, drain, 0)

    rw = rw_ref[...]
    acc = rw[:, 0:1] * buf[0]
    for k in range(1, TOP_K):
        acc = acc + rw[:, k:k + 1] * buf[k]
    out = x1_ref[...] + mod_ref[0][:, 5 * D_MODEL:6 * D_MODEL] * acc
    if final:
        out = _rms(out, nfin_ref[...])
    out_ref[...] = out


def _combine(offsets, tile_mod, route, x1, rw_t, mod, norm_final, ys, final):
    n = x1.shape[0]
    grid_spec = pltpu.PrefetchScalarGridSpec(
        num_scalar_prefetch=2,
        grid=(n // TM,),
        in_specs=[pl.BlockSpec((1, 1, 8 * TM), lambda i, off, tm: (i, 0, 0), memory_space=pltpu.SMEM),
                  pl.BlockSpec((TM, D_MODEL), lambda i, off, tm: (i, 0)),
                  pl.BlockSpec((TM, 8), lambda i, off, tm: (i, 0)),
                  pl.BlockSpec((1, 1, N_MOD * D_MODEL), lambda i, off, tm: (tm[i], 0, 0)),
                  pl.BlockSpec((1, D_MODEL), lambda i, off, tm: (0, 0)),
                  pl.BlockSpec(memory_space=pl.ANY)],
        out_specs=pl.BlockSpec((TM, D_MODEL), lambda i, off, tm: (i, 0)),
        scratch_shapes=[pltpu.VMEM((TOP_K, TM, D_MODEL), F32), pltpu.SemaphoreType.DMA],
    )
    return pl.pallas_call(
        functools.partial(_combine_kernel, final),
        grid_spec=grid_spec,
        out_shape=jax.ShapeDtypeStruct((n, D_MODEL), F32),
        compiler_params=pltpu.CompilerParams(
            dimension_semantics=("arbitrary",), vmem_limit_bytes=VMEM_LIMIT),
        name="moe_combine",
    )(offsets, tile_mod, route, x1, rw_t, mod, norm_final, ys)


def _moe(layer, h2, ri, rw, counts, x1, tile_mod, mod, norm_final, lw, final):
    n = h2.shape[0]
    n_tiles = n // TM
    slots = n * TOP_K + N_EXPERTS * TM
    cnt = counts[:, 0].astype(I32)
    padded = ((cnt + TM - 1) // TM) * TM
    ends = jnp.cumsum(padded)
    offsets = (ends - padded).astype(I32)
    n_used = (ends[-1] // TM).astype(I32)
    tile_ids = jnp.minimum(jnp.arange(slots // TM, dtype=I32), n_used - 1)
    tile_expert = jnp.sum((ends[None, :] // TM <= tile_ids[:, None]).astype(I32), axis=1)
    route = ri.reshape(8, n_tiles, TM).transpose(1, 0, 2).reshape(n_tiles, 1, 8 * TM)

    xs = _dispatch(offsets, route, h2, jnp.zeros((slots, D_MODEL), F32))
    ys = _ffn(layer, tile_expert, n_used.reshape(1), xs,
              lw["w_gate_up"], lw["b_gate_up"], lw["w_down"], lw["b_down"])
    return _combine(offsets, tile_mod, route, x1, rw.T, mod, norm_final, ys, final)


def _block_diag_pairs(w):
    z = jnp.zeros_like(w[:, 0::2])
    top = jnp.concatenate([w[:, 0::2], z], axis=-1)
    bot = jnp.concatenate([z, w[:, 1::2]], axis=-1)
    return jnp.concatenate([top, bot], axis=-2)


def kernel(x_prompt, x_sample, state_hgrn, state_rglru, c, c_ctx, norm_mix_w, norm_ffn_w, norm_final_w, w_mod, b_mod, w_in, hgrn_lb_logits, hgrn_norm_w, sconv_w, lru_conv_w, lru_conv_b, lru_wa, lru_ba, lru_wi, lru_bi, lru_lambda, w_branch, w_merge, b_merge, w_out, w_router, b_router, w_gate_up, b_gate_up, w_down, b_down):
    depth = w_in.shape[0]
    bp, tp, _ = x_prompt.shape
    bs, ts, _ = x_sample.shape
    assert tp == TM and ts % TM == 0 and bs + 1 <= MOD_ROWS
    n_ctx = bp * tp
    n_ctx_tiles = n_ctx // TM
    nt_lat = ts // TM

    x = jnp.concatenate([x_prompt.reshape(n_ctx, D_MODEL), x_sample.reshape(bs * ts, D_MODEL)], axis=0)
    n_tiles = x.shape[0] // TM

    lb_cum = jnp.cumsum(jax.nn.softmax(hgrn_lb_logits.astype(F32), axis=0), axis=0)
    lower_bounds = lb_cum - lb_cum[0:1]

    cc = jnp.zeros((MOD_ROWS, D_MODEL), F32).at[0].set(c_ctx).at[1:1 + bs].set(c)
    mod_all = _modulation(cc, w_mod, b_mod)
    tile_mod = jnp.concatenate([jnp.zeros((n_ctx_tiles,), I32),
                                1 + jnp.arange(bs * nt_lat, dtype=I32) // nt_lat])

    lat = jnp.arange(bs * nt_lat, dtype=I32)
    lat_b, lat_c = lat // nt_lat, lat % nt_lat
    ctx = jnp.arange(bp, dtype=I32)
    one = jnp.ones_like(ctx)
    tabs = (
        jnp.concatenate([ctx, n_ctx_tiles + lat]),
        jnp.concatenate([ctx, n_ctx_tiles + lat_b * nt_lat + (nt_lat - 1 - lat_c)]),
        jnp.concatenate([one, (lat_c == 0).astype(I32)]),
        jnp.concatenate([one, (lat_c == nt_lat - 1).astype(I32)]),
        jnp.concatenate([ctx, bp + lat_b]),
        jnp.concatenate([one * (TM - 1), jnp.full_like(lat, GRID_W - 1)]),
    )

    new_hgrn, new_lru = [], []
    for l in range(depth):
        lw = dict(
            norm_mix=norm_mix_w[l].reshape(1, D_MODEL), norm_ffn=norm_ffn_w[l].reshape(1, D_MODEL),
            w_merge=w_merge[l].astype(BF16), b_merge=b_merge[l].reshape(1, 3 * D_MODEL),
            w_branch=w_branch[l].astype(BF16), w_out=w_out[l].astype(BF16),
            hgrn_norm=hgrn_norm_w[l].reshape(1, W_MIX), sconv_w=sconv_w[l],
            w_router_t=w_router[l].T, b_router=b_router[l].reshape(N_EXPERTS, 1),
            w_gate_up=w_gate_up, b_gate_up=b_gate_up, w_down=w_down, b_down=b_down)
        mod = mod_all[l].reshape(MOD_ROWS, 1, N_MOD * D_MODEL)

        p = _premix(x, tile_mod, mod, lw["norm_mix"], w_in[l].astype(BF16))

        lru_par = jnp.concatenate(
            [lru_conv_w[l], lru_conv_b[l][None], lru_ba[l], lru_bi[l], lru_lambda[l],
             jnp.zeros((5, W_MIX), F32)], axis=0)
        w_lru = jnp.concatenate([_block_diag_pairs(lru_wa[l]), _block_diag_pairs(lru_wi[l])], axis=-1).astype(BF16)
        s0_hgrn = jnp.concatenate([jnp.zeros((bp,) + state_hgrn.shape[2:], F32), state_hgrn[:, l]], axis=0)
        s0_lru = jnp.concatenate([jnp.zeros((bp, 2, W_MIX), F32), state_rglru[:, l]], axis=0)
        o_f, hl_f, o_b, hl_b, s_hgrn, s_lru = _scans(p, tabs, lower_bounds[l], lru_par, w_lru, s0_hgrn, s0_lru)
        new_hgrn.append(s_hgrn[:bp])
        new_lru.append(s_lru[:bp])

        x1, h2, ri, rw, counts = _merge(x, p, (o_f, hl_f, o_b, hl_b), tile_mod, mod, lw, n_ctx_tiles)
        x = _moe(l, h2, ri, rw, counts, x1, tile_mod, mod, norm_final_w.reshape(1, D_MODEL), lw,
                 final=(l == depth - 1))

    y_prompt = x[:n_ctx].reshape(bp, tp, D_MODEL)
    y_sample = x[n_ctx:].reshape(bs, ts, D_MODEL)
    return (y_prompt, y_sample, jnp.stack(new_hgrn, axis=1), jnp.stack(new_lru, axis=1))
```

```python
import functools

import jax
import jax.numpy as jnp
from jax import lax
from jax.experimental import pallas as pl
from jax.experimental.pallas import tpu as pltpu

F32 = jnp.float32
BF16 = jnp.bfloat16
I32 = jnp.int32

D_MODEL = 1024
W_MIX = 512
N_HEADS = 4
D_HEAD = 128
N_IN = 10
N_MOD = 6
EPS = 1e-6
LRU_C = 8.0
N_EXPERTS = 32
TOP_K = 4
D_FF = 1024
SWIGLU_LIMIT = 7.0
SWIGLU_ALPHA = 1.702

TM = 256
CHUNK = 64
SUB = 16
N_SUB = CHUNK // SUB
SUB_SHIFT = SUB.bit_length() - 1
DIAG_GROUP = 4
N_CHUNK = TM // CHUNK
ROWS = 8
LOG2E = 1.4426950408889634
DMA_UNROLL = 8
GRID_W = 64
MOD_ROWS = 8
VMEM_LIMIT = 56 * 1024 * 1024


def _dot(a, b):
    return jnp.dot(a, b, preferred_element_type=F32)


def _dot_nt(a, b):
    return lax.dot_general(a, b, (((1,), (1,)), ((), ())), preferred_element_type=F32)


def _sigmoid(x):
    return 0.5 * jnp.tanh(0.5 * x) + 0.5


def _rms(x, w):
    return x * lax.rsqrt(jnp.mean(x * x, axis=-1, keepdims=True) + EPS) * w


def _split3(x):
    a = x.astype(BF16)
    r = x - a.astype(F32)
    b = r.astype(BF16)
    c = (r - b.astype(F32)).astype(BF16)
    return a, b, c


def _const_spec(shape):
    n = len(shape)
    return pl.BlockSpec(shape, lambda *_: (0,) * n, pipeline_mode=pl.Buffered(1))


def _mod_kernel(c_ref, w_ref, b_ref, o_ref):
    c = c_ref[...]
    s = (c * _sigmoid(c)).astype(BF16)
    o_ref[0] = _dot(s, w_ref[0].astype(BF16)) + b_ref[0]


def _modulation(cc, w_mod, b_mod):
    depth, _, width = w_mod.shape
    nb = 1536
    return pl.pallas_call(
        _mod_kernel,
        grid=(depth, width // nb),
        in_specs=[
            pl.BlockSpec((MOD_ROWS, D_MODEL), lambda l, j: (0, 0)),
            pl.BlockSpec((1, D_MODEL, nb), lambda l, j: (l, 0, j)),
            pl.BlockSpec((1, 1, nb), lambda l, j: (l, 0, j)),
        ],
        out_specs=pl.BlockSpec((1, MOD_ROWS, nb), lambda l, j: (l, 0, j)),
        out_shape=jax.ShapeDtypeStruct((depth, MOD_ROWS, width), F32),
        compiler_params=pltpu.CompilerParams(
            dimension_semantics=("arbitrary", "arbitrary"), vmem_limit_bytes=VMEM_LIMIT),
        name="modulation",
    )(cc, w_mod, b_mod.reshape(depth, 1, width))


def _premix_kernel(tmod_ref, x_ref, mod_ref, nw_ref, win_ref, p_ref):
    del tmod_ref
    mod = mod_ref[0]
    h = _rms(x_ref[...], nw_ref[...]) * (1.0 + mod[:, D_MODEL:2 * D_MODEL]) + mod[:, 0:D_MODEL]
    p_ref[...] = _dot(h.astype(BF16), win_ref[...])


def _premix(x, tile_mod, mod, norm_w, w_in_bf):
    n = x.shape[0]
    width = w_in_bf.shape[1]
    grid_spec = pltpu.PrefetchScalarGridSpec(
        num_scalar_prefetch=1,
        grid=(n // TM,),
        in_specs=[
            pl.BlockSpec((TM, D_MODEL), lambda i, tm: (i, 0)),
            pl.BlockSpec((1, 1, N_MOD * D_MODEL), lambda i, tm: (tm[i], 0, 0)),
            _const_spec((1, D_MODEL)),
            _const_spec((D_MODEL, width)),
        ],
        out_specs=pl.BlockSpec((TM, width), lambda i, tm: (i, 0)),
    )
    return pl.pallas_call(
        _premix_kernel,
        grid_spec=grid_spec,
        out_shape=jax.ShapeDtypeStruct((n, width), F32),
        compiler_params=pltpu.CompilerParams(
            dimension_semantics=("arbitrary",), vmem_limit_bytes=VMEM_LIMIT),
        name="premix",
    )(tile_mod, x, mod, norm_w, w_in_bf)


def _shift_rows(x, s):
    return pltpu.roll(x, s % x.shape[0], axis=0)


def _lru_scan(a, b, h0, reverse):
    n, width = a.shape
    nblk = n // ROWS
    a = a.reshape(nblk, ROWS, width)
    b = b.reshape(nblk, ROWS, width)
    row = lax.broadcasted_iota(I32, a.shape, 1)
    s = 1
    while s < ROWS:
        keep = (row < ROWS - s) if reverse else (row >= s)
        shift = (ROWS - s) if reverse else s
        a_sh = jnp.where(keep, pltpu.roll(a, shift, axis=1), 1.0)
        b_sh = jnp.where(keep, pltpu.roll(b, shift, axis=1), 0.0)
        b = a * b_sh + b
        a = a * a_sh
        s *= 2
    edge = 0 if reverse else ROWS - 1
    out = [None] * nblk
    carry = jnp.broadcast_to(h0, (ROWS, width))
    for j in (reversed(range(nblk)) if reverse else range(nblk)):
        hj = a[j] * carry + b[j]
        out[j] = hj
        carry = jnp.broadcast_to(hj[edge:edge + 1, :], (ROWS, width))
    return jnp.concatenate(out, axis=0)


def _pair_blocks(x):
    zero = jnp.zeros((x.shape[0], D_HEAD), x.dtype)
    return jnp.concatenate([jnp.concatenate([x[:, :D_HEAD], zero], axis=1),
                            jnp.concatenate([zero, x[:, D_HEAD:]], axis=1)], axis=0)


def _scan_kernel(tf_ref, tb_ref, first_ref, last_ref, sb_ref, lm1_ref,
                 qf_ref, ff_ref, vf_ref, rxf_ref, qb_ref, fb_ref, vb_ref, rxb_ref,
                 lb_ref, lp_ref, wlru_ref, s0h_ref, s0l_ref,
                 of_ref, hlf_ref, ob_ref, hlb_ref, sfh_ref, sfl_ref,
                 st_ref, hcar_ref, qs_ref, ks_ref, gs_ref, lcs_ref):
    del tf_ref, tb_ref, sb_ref
    step = pl.program_id(0)

    @pl.when(first_ref[step] == 1)
    def _init():
        for d in range(2):
            for h in range(N_HEADS):
                st_ref[d, h] = s0h_ref[0, d, h].T
        hcar_ref[0:2, :] = s0l_ref[0]

    lm1 = lm1_ref[step]
    row = lax.broadcasted_iota(I32, (TM, W_MIX), 0)
    pos = row & lm1

    dirs = ((qf_ref, ff_ref, rxf_ref, hlf_ref), (qb_ref, fb_ref, rxb_ref, hlb_ref))
    for d, (q_ref, f_ref, rx_ref, hl_ref) in enumerate(dirs):
        q = q_ref[...]
        qs_ref[d] = q * _sigmoid(q) * (D_HEAD ** -0.5)
        f = f_ref[...]
        e = jnp.exp(-jnp.abs(f))
        r = 1.0 / (1.0 + e)
        sig_neg = jnp.where(f >= 0, e * r, r)
        log_sig = jnp.minimum(f, 0.0) + jnp.log(r)
        lb = lb_ref[d:d + 1, :]
        ks_ref[d] = (1.0 - lb) * sig_neg
        la = jnp.log(lb)
        lbb = jnp.log1p(-lb) + log_sig
        gs_ref[d] = jnp.maximum(la, lbb) + jnp.log(1.0 + jnp.exp(-jnp.abs(la - lbb)))

        u = rx_ref[...]
        xr = (lp_ref[2:3, :] * u + lp_ref[4:5, :]
              + lp_ref[0:1, :] * jnp.where(pos >= 2, _shift_rows(u, 2), 0.0)
              + lp_ref[1:2, :] * jnp.where(pos >= 1, _shift_rows(u, 1), 0.0)
              + lp_ref[3:4, :] * jnp.where(pos < lm1, _shift_rows(u, -1), 0.0))
        rl, il = [], []
        for nblk in range(W_MIX // 128):
            z = _dot(xr[:, 128 * nblk:128 * (nblk + 1)].astype(BF16), wlru_ref[d, nblk])
            rl.append(z[:, :128])
            il.append(z[:, 128:])
        r_gate = _sigmoid(jnp.concatenate(rl, axis=1) + lp_ref[5 + d:6 + d, :])
        i_gate = _sigmoid(jnp.concatenate(il, axis=1) + lp_ref[7 + d:8 + d, :])
        lam = lp_ref[9 + d:10 + d, :]
        softplus = jnp.maximum(-lam, 0.0) + jnp.log1p(jnp.exp(-jnp.abs(lam)))
        log_a = (-LRU_C) * r_gate * softplus
        a = jnp.exp(log_a)
        th = jnp.tanh(log_a)
        bx = jnp.sqrt(-2.0 * th / (1.0 - th)) * i_gate * xr
        hd = _lru_scan(a, bx, hcar_ref[d:d + 1, :], reverse=(d == 1))
        hl_ref[...] = hd
        edge = TM - 1 if d == 0 else 0
        hcar_ref[d:d + 1, :] = hd[edge:edge + 1, :]

    r64 = lax.broadcasted_iota(I32, (CHUNK, CHUNK), 0)
    c64 = lax.broadcasted_iota(I32, (CHUNK, CHUNK), 1)
    same_sub = (r64 >> SUB_SHIFT) == (c64 >> SUB_SHIFT)
    hrow = lax.broadcasted_iota(I32, (CHUNK // 2, 2 * CHUNK), 0)
    hcol = lax.broadcasted_iota(I32, (CHUNK // 2, 2 * CHUNK), 1) & (CHUNK - 1)
    hrel = hcol - ((hrow >> 3) << SUB_SHIFT)
    hmod = hrow & (ROWS - 1)
    v_refs = (vf_ref, vb_ref)
    o_refs = (of_ref, ob_ref)

    def chunk_body(ci, carry):
        for d in range(2):
            rev = d == 1
            c0 = pl.multiple_of((N_CHUNK - 1 - ci) * CHUNK if rev else ci * CHUNK, CHUNK)
            order = tuple(reversed(range(N_SUB))) if rev else tuple(range(N_SUB))
            q = qs_ref[d, pl.ds(c0, CHUNK), :]
            k = ks_ref[d, pl.ds(c0, CHUNK), :]
            g = gs_ref[d, pl.ds(c0, CHUNK), :]
            v = v_refs[d][pl.ds(c0, CHUNK), :]

            tri = (same_sub & ((c64 >= r64) if rev else (c64 <= r64))).astype(BF16)
            res = _dot(tri, jnp.concatenate(_split3(g), axis=1))
            lc = res[:, 0:W_MIX] + res[:, W_MIX:2 * W_MIX] + res[:, 2 * W_MIX:3 * W_MIX]
            lc2 = lc * LOG2E
            lcs_ref[d] = lc2
            last = 0 if rev else SUB - 1
            tot_sub = [lc[SUB * a + last:SUB * a + last + 1, :] for a in range(N_SUB)]
            totb = jnp.concatenate([jnp.broadcast_to(t, (SUB, W_MIX)) for t in tot_sub], axis=0)

            qh = q * jnp.exp2(lc2)
            kt = k * jnp.exp(totb - lc)

            tot = [tot_sub[a] for a in order]
            cum = [tot[0]]
            for i in range(1, N_SUB):
                cum.append(cum[-1] + tot[i])

            def sub_rows(x, a):
                return x[SUB * a:SUB * (a + 1), :]

            def by_time(fn):
                rows = [None] * N_SUB
                for i, a in enumerate(order):
                    rows[a] = fn(i, a)
                return jnp.concatenate(rows, axis=0)

            zeros_sub = jnp.zeros((SUB, W_MIX), F32)
            qe = by_time(lambda i, a: sub_rows(qh, a) if i == 0 else sub_rows(qh, a) * jnp.exp(cum[i - 1]))
            kl = by_time(lambda j, a: sub_rows(kt, a) if j == N_SUB - 1
                         else sub_rows(kt, a) * jnp.exp(cum[-1] - cum[j]))
            chunk_decay = jnp.exp(cum[-1])
            lhs, kts = [], []
            for j in range(N_SUB - 1):
                def lhs_rows(i, a, j=j):
                    if i <= j:
                        return zeros_sub
                    if i == j + 1:
                        return sub_rows(qh, a)
                    return sub_rows(qh, a) * jnp.exp(cum[i - 1] - cum[j])
                lhs.append(by_time(lhs_rows).astype(BF16))
                kts.append(by_time(lambda i, a, j=j: sub_rows(kt, a) if i == j else zeros_sub).astype(BF16))

            kb = k.astype(BF16)
            vb = v.astype(BF16)
            zero_state = jnp.zeros((D_HEAD, D_HEAD), F32)
            outs = []
            for hp in range(N_HEADS // 2):
                sl = slice(2 * D_HEAD * hp, 2 * D_HEAD * (hp + 1))
                k_pair = _pair_blocks(kb[:, sl])
                att_half = [jnp.zeros((CHUNK // 2, 2 * CHUNK), F32) for _ in range(2)]
                for s0 in range(0, SUB, DIAG_GROUP):
                    pieces, spans = [], []
                    for s in range(s0, s0 + DIAG_GROUP):
                        use = ((True, s >= ROWS) if rev else (s < ROWS, True))
                        for half in range(2):
                            if not use[half]:
                                continue
                            spans.append((s, half, len(pieces) * ROWS))
                            for a in range(N_SUB):
                                r0 = SUB * a + ROWS * half
                                lsel = jnp.broadcast_to(lcs_ref[d, SUB * a + s:SUB * a + s + 1, sl],
                                                        (ROWS, 2 * D_HEAD))
                                pieces.append(q[r0:r0 + ROWS, sl] * jnp.exp2(lc2[r0:r0 + ROWS, sl] - lsel))
                    col = _dot_nt(jnp.concatenate(pieces, axis=0).astype(BF16), k_pair)
                    for s, half, off in spans:
                        rm = hmod + ROWS * half
                        dmask = (hrel == s) & ((rm <= s) if rev else (rm >= s))
                        att_half[half] = jnp.where(dmask, col[off:off + CHUNK // 2], att_half[half])
                att = jnp.concatenate(
                    [att_half[half][ROWS * a:ROWS * (a + 1)] for a in range(N_SUB) for half in range(2)], axis=0)
                for j in range(N_SUB - 1):
                    att = att + _dot_nt(lhs[j][:, sl], _pair_blocks(kts[j][:, sl]))
                st = [st_ref[d, 2 * hp], st_ref[d, 2 * hp + 1]]
                st_pair = jnp.concatenate([jnp.concatenate([st[0], zero_state], axis=1),
                                           jnp.concatenate([zero_state, st[1]], axis=1)], axis=0)
                outs.append(_dot_nt(qe[:, sl].astype(BF16), st_pair.astype(BF16))
                            + _dot(att.astype(BF16), _pair_blocks(vb[:, sl])))
                for i in range(2):
                    hl = slice(D_HEAD * (2 * hp + i), D_HEAD * (2 * hp + i + 1))
                    st_ref[d, 2 * hp + i] = (st[i] * chunk_decay[:, hl]
                                             + _dot(v[:, hl].T.astype(BF16), kl[:, hl].astype(BF16)))
            o_refs[d][pl.ds(c0, CHUNK), :] = jnp.concatenate(outs, axis=1)
        return carry

    lax.fori_loop(0, N_CHUNK, chunk_body, 0)

    @pl.when(last_ref[step] == 1)
    def _final():
        for d in range(2):
            for h in range(N_HEADS):
                sfh_ref[0, d, h] = st_ref[d, h].T
        sfl_ref[0] = hcar_ref[0:2, :]


def _scans(p, tabs, lb, lru_par, w_lru, s0_hgrn, s0_lru):
    n = p.shape[0]
    n_steps = tabs[0].shape[0]
    n_seq = s0_hgrn.shape[0]

    def col(c, which):
        if which == 0:
            return pl.BlockSpec((TM, W_MIX), lambda s, tf, tb, fi, la, sb, lm: (tf[s], c))
        return pl.BlockSpec((TM, W_MIX), lambda s, tf, tb, fi, la, sb, lm: (tb[s], c))

    def out_col(which):
        if which == 0:
            return pl.BlockSpec((TM, W_MIX), lambda s, tf, tb, fi, la, sb, lm: (tf[s], 0))
        return pl.BlockSpec((TM, W_MIX), lambda s, tf, tb, fi, la, sb, lm: (tb[s], 0))

    st_spec = pl.BlockSpec((1, 2, N_HEADS, D_HEAD, D_HEAD),
                           lambda s, tf, tb, fi, la, sb, lm: (sb[s], 0, 0, 0, 0))
    sl_spec = pl.BlockSpec((1, 2, W_MIX), lambda s, tf, tb, fi, la, sb, lm: (sb[s], 0, 0))
    grid_spec = pltpu.PrefetchScalarGridSpec(
        num_scalar_prefetch=6,
        grid=(n_steps,),
        in_specs=[col(0, 0), col(1, 0), col(3, 0), col(8, 0),
                  col(0, 1), col(2, 1), col(3, 1), col(8, 1),
                  _const_spec((2, W_MIX)), _const_spec((16, W_MIX)),
                  _const_spec((2, W_MIX // 128, 128, 256)),
                  st_spec, sl_spec],
        out_specs=[out_col(0), out_col(0), out_col(1), out_col(1), st_spec, sl_spec],
        scratch_shapes=[
            pltpu.VMEM((2, N_HEADS, D_HEAD, D_HEAD), F32),
            pltpu.VMEM((8, W_MIX), F32),
            pltpu.VMEM((2, TM, W_MIX), F32),
            pltpu.VMEM((2, TM, W_MIX), F32),
            pltpu.VMEM((2, TM, W_MIX), F32),
            pltpu.VMEM((2, CHUNK, W_MIX), F32),
        ],
    )
    row = jax.ShapeDtypeStruct((n, W_MIX), F32)
    return pl.pallas_call(
        _scan_kernel,
        grid_spec=grid_spec,
        out_shape=[row, row, row, row,
                   jax.ShapeDtypeStruct((n_seq, 2, N_HEADS, D_HEAD, D_HEAD), F32),
                   jax.ShapeDtypeStruct((n_seq, 2, W_MIX), F32)],
        compiler_params=pltpu.CompilerParams(
            dimension_semantics=("arbitrary",), vmem_limit_bytes=VMEM_LIMIT),
        name="scans",
    )(*tabs, p, p, p, p, p, p, p, p, lb, lru_par, w_lru, s0_hgrn, s0_lru)


def _merge_kernel(n_ctx_tiles, tmod_ref,
                  x_ref, go_ref, sb_ref, sc_ref, sx_ref, rg_ref, of_ref, ob_ref, hlf_ref, hlb_ref,
                  mod_ref, nmix_ref, wmerge_ref, bmerge_ref, wbr_ref, wout_ref, hnorm_ref, scw_ref,
                  nffn_ref, wrt_ref, br_ref,
                  x1_ref, h2_ref, ri_ref, rw_ref, cnt_ref, base_ref):
    del tmod_ref
    i = pl.program_id(0)

    @pl.when(i == 0)
    def _init():
        base_ref[...] = jnp.zeros_like(base_ref)

    mod = mod_ref[0]
    x = x_ref[...]
    h = _rms(x, nmix_ref[...]) * (1.0 + mod[:, D_MODEL:2 * D_MODEL]) + mod[:, 0:D_MODEL]
    gates = _sigmoid(_dot(h.astype(BF16), wmerge_ref[...]) + bmerge_ref[...])

    o = of_ref[...] + ob_ref[...]
    go = go_ref[...]
    parts = []
    for hh in range(N_HEADS):
        sl = slice(D_HEAD * hh, D_HEAD * (hh + 1))
        parts.append(_rms(o[:, sl], hnorm_ref[:, sl]))
    ya = jnp.concatenate(parts, axis=1) * (go * _sigmoid(go))

    lm1 = jnp.where(i < n_ctx_tiles, TM - 1, GRID_W - 1)
    pos = lax.broadcasted_iota(I32, (TM, W_MIX), 0) & lm1
    u = sc_ref[...] * sx_ref[...]
    conv = (scw_ref[1:2, :] * u
            + scw_ref[0:1, :] * jnp.where(pos >= 1, _shift_rows(u, 1), 0.0)
            + scw_ref[2:3, :] * jnp.where(pos < lm1, _shift_rows(u, -1), 0.0))
    yb = sb_ref[...] * conv

    yc = (hlf_ref[...] + hlb_ref[...]) * jax.nn.gelu(rg_ref[...])

    merged = (gates[:, 0:D_MODEL] * _dot(ya.astype(BF16), wbr_ref[0])
              + gates[:, D_MODEL:2 * D_MODEL] * _dot(yb.astype(BF16), wbr_ref[1])
              + gates[:, 2 * D_MODEL:3 * D_MODEL] * _dot(yc.astype(BF16), wbr_ref[2]))
    y = _dot(merged.astype(BF16), wout_ref[...])
    x1 = x + mod[:, 2 * D_MODEL:3 * D_MODEL] * y
    x1_ref[...] = x1

    h2 = _rms(x1, nffn_ref[...]) * (1.0 + mod[:, 4 * D_MODEL:5 * D_MODEL]) + mod[:, 3 * D_MODEL:4 * D_MODEL]
    h2_ref[...] = h2

    h_hi = h2.astype(BF16)
    h_lo = (h2 - h_hi.astype(F32)).astype(BF16)
    wr = wrt_ref[...]
    w_hi = wr.astype(BF16)
    w_lo = (wr - w_hi.astype(F32)).astype(BF16)
    logits = _dot_nt(w_hi, h_hi) + _dot_nt(w_hi, h_lo) + _dot_nt(w_lo, h_hi) + br_ref[...]

    erow = lax.broadcasted_iota(I32, (N_EXPERTS, TM), 0).astype(F32)
    r8 = lax.broadcasted_iota(I32, (8, TM), 0)
    tri = (lax.broadcasted_iota(I32, (TM, TM), 0) <= lax.broadcasted_iota(I32, (TM, TM), 1)).astype(BF16)
    base = base_ref[...]
    top_v, hots, idxs = [], [], []
    work = logits
    for _ in range(TOP_K):
        m = jnp.max(work, axis=0, keepdims=True)
        idx = jnp.min(jnp.where(work == m, erow, float(N_EXPERTS)), axis=0, keepdims=True)
        hot = erow == idx
        work = jnp.where(hot, -jnp.inf, work)
        top_v.append(m)
        hots.append(hot)
        idxs.append(idx)
    sel = hots[0] | hots[1] | hots[2] | hots[3]
    selb = sel.astype(BF16)
    incl = _dot(selb, tri)
    total = _dot(selb, jnp.ones((TM, TM), BF16))
    rank_all = base + incl - 1.0
    ex = [jnp.exp(v - top_v[0]) for v in top_v]
    denom = ex[0] + ex[1] + ex[2] + ex[3]
    ri = jnp.zeros((8, TM), F32)
    rw = jnp.zeros((8, TM), F32)
    for k in range(TOP_K):
        rank_k = jnp.sum(jnp.where(hots[k], rank_all, 0.0), axis=0, keepdims=True)
        ri = jnp.where(r8 == k, idxs[k], ri)
        ri = jnp.where(r8 == TOP_K + k, rank_k, ri)
        rw = jnp.where(r8 == k, ex[k] / denom, rw)
    ri_ref[...] = ri.astype(I32)
    rw_ref[...] = rw
    base = base + total
    base_ref[...] = base
    cnt_ref[...] = base[:, 0:128]


def _merge(x, p, scan_out, tile_mod, mod, lw, n_ctx_tiles):
    n = x.shape[0]
    o_f, hl_f, o_b, hl_b = scan_out

    def tok(width):
        return pl.BlockSpec((TM, width), lambda i, tm: (i, 0))

    def pcol(c):
        return pl.BlockSpec((TM, W_MIX), lambda i, tm: (i, c))

    grid_spec = pltpu.PrefetchScalarGridSpec(
        num_scalar_prefetch=1,
        grid=(n // TM,),
        in_specs=[tok(D_MODEL), pcol(4), pcol(5), pcol(6), pcol(7), pcol(9),
                  tok(W_MIX), tok(W_MIX), tok(W_MIX), tok(W_MIX),
                  pl.BlockSpec((1, 1, N_MOD * D_MODEL), lambda i, tm: (tm[i], 0, 0)),
                  _const_spec((1, D_MODEL)),
                  _const_spec((D_MODEL, 3 * D_MODEL)), _const_spec((1, 3 * D_MODEL)),
                  _const_spec((3, W_MIX, D_MODEL)), _const_spec((D_MODEL, D_MODEL)),
                  _const_spec((1, W_MIX)), _const_spec((3, W_MIX)),
                  _const_spec((1, D_MODEL)), _const_spec((N_EXPERTS, D_MODEL)),
                  _const_spec((N_EXPERTS, 1))],
        out_specs=[tok(D_MODEL), tok(D_MODEL),
                   pl.BlockSpec((8, TM), lambda i, tm: (0, i)),
                   pl.BlockSpec((8, TM), lambda i, tm: (0, i)),
                   pl.BlockSpec((N_EXPERTS, 128), lambda i, tm: (0, 0))],
        scratch_shapes=[pltpu.VMEM((N_EXPERTS, TM), F32)],
    )
    return pl.pallas_call(
        functools.partial(_merge_kernel, n_ctx_tiles),
        grid_spec=grid_spec,
        out_shape=[jax.ShapeDtypeStruct((n, D_MODEL), F32), jax.ShapeDtypeStruct((n, D_MODEL), F32),
                   jax.ShapeDtypeStruct((8, n), I32), jax.ShapeDtypeStruct((8, n), F32),
                   jax.ShapeDtypeStruct((N_EXPERTS, 128), F32)],
        compiler_params=pltpu.CompilerParams(
            dimension_semantics=("arbitrary",), vmem_limit_bytes=VMEM_LIMIT),
        name="merge_router",
    )(tile_mod, x, p, p, p, p, p, o_f, o_b, hl_f, hl_b, mod,
      lw["norm_mix"], lw["w_merge"], lw["b_merge"], lw["w_branch"], lw["w_out"],
      lw["hgrn_norm"], lw["sconv_w"], lw["norm_ffn"], lw["w_router_t"], lw["b_router"])


def _row_copy(src, src_row, dst, dst_row, sem):
    return pltpu.make_async_copy(src.at[pl.ds(src_row, 1)], dst.at[pl.ds(dst_row, 1)], sem)


def _dispatch_kernel(ztile_ref, slot_ref, h2_ref, xs_ref, zeros_ref, sem):
    @pl.when(pl.program_id(0) == 0)
    def _zero_group_tails():
        zeros_ref[...] = jnp.zeros_like(zeros_ref)
        for e in range(2 * N_EXPERTS):
            @pl.when(ztile_ref[e] >= 0)
            def _start(e=e):
                start = pl.multiple_of(ztile_ref[e], TM)
                pltpu.make_async_copy(zeros_ref, xs_ref.at[pl.ds(start, TM)], sem).start()
        for e in range(2 * N_EXPERTS):
            @pl.when(ztile_ref[e] >= 0)
            def _wait():
                pltpu.make_async_copy(zeros_ref, xs_ref.at[pl.ds(0, TM)], sem).wait()

    def issue(r, carry):
        for k in range(TOP_K):
            _row_copy(h2_ref, r, xs_ref, slot_ref[0, 0, k * TM + r], sem).start()
        return carry

    lax.fori_loop(0, TM, issue, 0, unroll=DMA_UNROLL)

    def drain(r, carry):
        for k in range(TOP_K):
            _row_copy(h2_ref, 0, xs_ref, 0, sem).wait()
        return carry

    lax.fori_loop(0, TM, drain, 0, unroll=DMA_UNROLL)


def _dispatch(zero_tiles, slots, h2, n_slots):
    n = h2.shape[0]
    grid_spec = pltpu.PrefetchScalarGridSpec(
        num_scalar_prefetch=1,
        grid=(n // TM,),
        in_specs=[pl.BlockSpec((1, 1, TOP_K * TM), lambda i, zt: (i, 0, 0), memory_space=pltpu.SMEM),
                  pl.BlockSpec((TM, D_MODEL), lambda i, zt: (i, 0))],
        out_specs=pl.BlockSpec(memory_space=pl.ANY),
        scratch_shapes=[pltpu.VMEM((TM, D_MODEL), F32), pltpu.SemaphoreType.DMA],
    )
    return pl.pallas_call(
        _dispatch_kernel,
        grid_spec=grid_spec,
        out_shape=jax.ShapeDtypeStruct((n_slots, D_MODEL), F32),
        compiler_params=pltpu.CompilerParams(dimension_semantics=("arbitrary",)),
        name="moe_dispatch",
    )(zero_tiles, slots, h2)


def _ffn_kernel(te_ref, nused_ref, xs_ref, wgu_ref, bgu_ref, wd_ref, bd_ref, ys_ref, wgu_bf, wd_bf):
    i = pl.program_id(0)
    used = i < nused_ref[0]

    @pl.when(used)
    def _compute():
        prev = te_ref[jnp.maximum(i - 1, 0)]

        @pl.when((i == 0) | (te_ref[i] != prev))
        def _cast_weights():
            wgu_bf[...] = wgu_ref[0, 0].astype(BF16)
            wd_bf[...] = wd_ref[0, 0].astype(BF16)

        gu = _dot(xs_ref[...].astype(BF16), wgu_bf[...]) + bgu_ref[0, 0]
        glu = jnp.minimum(gu[:, 0:D_FF], SWIGLU_LIMIT)
        lin = jnp.clip(gu[:, D_FF:2 * D_FF], -SWIGLU_LIMIT, SWIGLU_LIMIT)
        act = glu * _sigmoid(SWIGLU_ALPHA * glu) * (lin + 1.0)
        ys_ref[...] = _dot(act.astype(BF16), wd_bf[...]) + bd_ref[0, 0]

    @pl.when(jnp.logical_not(used))
    def _skip():
        ys_ref[...] = jnp.zeros_like(ys_ref)


def _ffn(layer, tile_expert, n_used, xs, w_gate_up, b_gate_up, w_down, b_down):
    rows = xs.shape[0]
    depth = w_gate_up.shape[0]
    grid_spec = pltpu.PrefetchScalarGridSpec(
        num_scalar_prefetch=2,
        grid=(rows // TM,),
        in_specs=[
            pl.BlockSpec((TM, D_MODEL), lambda i, te, nu: (jnp.minimum(i, nu[0] - 1), 0)),
            pl.BlockSpec((1, 1, D_MODEL, 2 * D_FF), lambda i, te, nu: (layer, te[i], 0, 0)),
            pl.BlockSpec((1, 1, 1, 2 * D_FF), lambda i, te, nu: (layer, te[i], 0, 0)),
            pl.BlockSpec((1, 1, D_FF, D_MODEL), lambda i, te, nu: (layer, te[i], 0, 0)),
            pl.BlockSpec((1, 1, 1, D_MODEL), lambda i, te, nu: (layer, te[i], 0, 0)),
        ],
        out_specs=pl.BlockSpec((TM, D_MODEL), lambda i, te, nu: (i, 0)),
        scratch_shapes=[pltpu.VMEM((D_MODEL, 2 * D_FF), BF16), pltpu.VMEM((D_FF, D_MODEL), BF16)],
    )
    return pl.pallas_call(
        _ffn_kernel,
        grid_spec=grid_spec,
        out_shape=jax.ShapeDtypeStruct((rows, D_MODEL), F32),
        compiler_params=pltpu.CompilerParams(
            dimension_semantics=("arbitrary",), vmem_limit_bytes=VMEM_LIMIT),
        name="moe_ffn",
    )(tile_expert, n_used, xs, w_gate_up, b_gate_up.reshape(depth, N_EXPERTS, 1, 2 * D_FF),
      w_down, b_down.reshape(depth, N_EXPERTS, 1, D_MODEL))


def _combine_kernel(final, tmod_ref, slot_ref, x1_ref, rw_ref, mod_ref, nfin_ref, ys_ref,
                    out_ref, buf, sem):
    del tmod_ref

    def issue(r, carry):
        for k in range(TOP_K):
            pltpu.make_async_copy(ys_ref.at[pl.ds(slot_ref[0, 0, k * TM + r], 1)],
                                  buf.at[k, pl.ds(r, 1)], sem).start()
        return carry

    lax.fori_loop(0, TM, issue, 0, unroll=DMA_UNROLL)

    def drain(r, carry):
        for k in range(TOP_K):
            pltpu.make_async_copy(ys_ref.at[pl.ds(0, 1)], buf.at[0, pl.ds(0, 1)], sem).wait()
        return carry

    lax.fori_loop(0, TM, drain, 0, unroll=DMA_UNROLL)

    rw = rw_ref[...]
    acc = rw[:, 0:1] * buf[0]
    for k in range(1, TOP_K):
        acc = acc + rw[:, k:k + 1] * buf[k]
    out = x1_ref[...] + mod_ref[0][:, 5 * D_MODEL:6 * D_MODEL] * acc
    if final:
        out = _rms(out, nfin_ref[...])
    out_ref[...] = out


def _combine(tile_mod, slots, x1, rw_t, mod, norm_final, ys, final):
    n = x1.shape[0]
    grid_spec = pltpu.PrefetchScalarGridSpec(
        num_scalar_prefetch=1,
        grid=(n // TM,),
        in_specs=[pl.BlockSpec((1, 1, TOP_K * TM), lambda i, tm: (i, 0, 0), memory_space=pltpu.SMEM),
                  pl.BlockSpec((TM, D_MODEL), lambda i, tm: (i, 0)),
                  pl.BlockSpec((TM, 8), lambda i, tm: (i, 0)),
                  pl.BlockSpec((1, 1, N_MOD * D_MODEL), lambda i, tm: (tm[i], 0, 0)),
                  pl.BlockSpec((1, D_MODEL), lambda i, tm: (0, 0)),
                  pl.BlockSpec(memory_space=pl.ANY)],
        out_specs=pl.BlockSpec((TM, D_MODEL), lambda i, tm: (i, 0)),
        scratch_shapes=[pltpu.VMEM((TOP_K, TM, D_MODEL), F32), pltpu.SemaphoreType.DMA],
    )
    return pl.pallas_call(
        functools.partial(_combine_kernel, final),
        grid_spec=grid_spec,
        out_shape=jax.ShapeDtypeStruct((n, D_MODEL), F32),
        compiler_params=pltpu.CompilerParams(
            dimension_semantics=("arbitrary",), vmem_limit_bytes=VMEM_LIMIT),
        name="moe_combine",
    )(tile_mod, slots, x1, rw_t, mod, norm_final, ys)


def _moe(layer, h2, ri, rw, counts, x1, tile_mod, mod, norm_final, lw, final):
    n = h2.shape[0]
    n_tiles = n // TM
    n_slots = n * TOP_K + N_EXPERTS * TM
    cnt = counts[:, 0].astype(I32)
    padded = ((cnt + TM - 1) // TM) * TM
    ends = jnp.cumsum(padded)
    offsets = (ends - padded).astype(I32)
    n_used = (ends[-1] // TM).astype(I32)
    tile_ids = jnp.minimum(jnp.arange(n_slots // TM, dtype=I32), n_used - 1)
    tile_expert = jnp.sum((ends[None, :] // TM <= tile_ids[:, None]).astype(I32), axis=1)
    onehot = ri[0:TOP_K, :, None] == jnp.arange(N_EXPERTS, dtype=I32)
    slot = jnp.sum(jnp.where(onehot, offsets, 0), axis=-1) + ri[TOP_K:2 * TOP_K]
    slots = slot.reshape(TOP_K, n_tiles, TM).transpose(1, 0, 2).reshape(n_tiles, 1, TOP_K * TM)
    tail = n_used + jnp.arange(N_EXPERTS, dtype=I32)
    zero_tiles = jnp.concatenate([jnp.where(padded > 0, ends - TM, -1),
                                  jnp.where(tail < n_slots // TM, tail * TM, -1)]).astype(I32)

    xs = _dispatch(zero_tiles, slots, h2, n_slots)
    ys = _ffn(layer, tile_expert, n_used.reshape(1), xs,
              lw["w_gate_up"], lw["b_gate_up"], lw["w_down"], lw["b_down"])
    return _combine(tile_mod, slots, x1, rw.T, mod, norm_final, ys, final)


def _block_diag_pairs(w):
    z = jnp.zeros_like(w[:, 0::2])
    top = jnp.concatenate([w[:, 0::2], z], axis=-1)
    bot = jnp.concatenate([z, w[:, 1::2]], axis=-1)
    return jnp.concatenate([top, bot], axis=-2)


def kernel(x_prompt, x_sample, state_hgrn, state_rglru, c, c_ctx, norm_mix_w, norm_ffn_w, norm_final_w, w_mod, b_mod, w_in, hgrn_lb_logits, hgrn_norm_w, sconv_w, lru_conv_w, lru_conv_b, lru_wa, lru_ba, lru_wi, lru_bi, lru_lambda, w_branch, w_merge, b_merge, w_out, w_router, b_router, w_gate_up, b_gate_up, w_down, b_down):
    depth = w_in.shape[0]
    bp, tp, _ = x_prompt.shape
    bs, ts, _ = x_sample.shape
    assert tp == TM and ts % TM == 0 and bs + 1 <= MOD_ROWS
    n_ctx = bp * tp
    n_ctx_tiles = n_ctx // TM
    nt_lat = ts // TM

    x = jnp.concatenate([x_prompt.reshape(n_ctx, D_MODEL), x_sample.reshape(bs * ts, D_MODEL)], axis=0)
    n_tiles = x.shape[0] // TM

    lb_cum = jnp.cumsum(jax.nn.softmax(hgrn_lb_logits.astype(F32), axis=0), axis=0)
    lower_bounds = lb_cum - lb_cum[0:1]

    cc = jnp.zeros((MOD_ROWS, D_MODEL), F32).at[0].set(c_ctx).at[1:1 + bs].set(c)
    mod_all = _modulation(cc, w_mod, b_mod)
    tile_mod = jnp.concatenate([jnp.zeros((n_ctx_tiles,), I32),
                                1 + jnp.arange(bs * nt_lat, dtype=I32) // nt_lat])

    lat = jnp.arange(bs * nt_lat, dtype=I32)
    lat_b, lat_c = lat // nt_lat, lat % nt_lat
    ctx = jnp.arange(bp, dtype=I32)
    one = jnp.ones_like(ctx)
    tabs = (
        jnp.concatenate([ctx, n_ctx_tiles + lat]),
        jnp.concatenate([ctx, n_ctx_tiles + lat_b * nt_lat + (nt_lat - 1 - lat_c)]),
        jnp.concatenate([one, (lat_c == 0).astype(I32)]),
        jnp.concatenate([one, (lat_c == nt_lat - 1).astype(I32)]),
        jnp.concatenate([ctx, bp + lat_b]),
        jnp.concatenate([one * (TM - 1), jnp.full_like(lat, GRID_W - 1)]),
    )

    new_hgrn, new_lru = [], []
    for l in range(depth):
        lw = dict(
            norm_mix=norm_mix_w[l].reshape(1, D_MODEL), norm_ffn=norm_ffn_w[l].reshape(1, D_MODEL),
            w_merge=w_merge[l].astype(BF16), b_merge=b_merge[l].reshape(1, 3 * D_MODEL),
            w_branch=w_branch[l].astype(BF16), w_out=w_out[l].astype(BF16),
            hgrn_norm=hgrn_norm_w[l].reshape(1, W_MIX), sconv_w=sconv_w[l],
            w_router_t=w_router[l].T, b_router=b_router[l].reshape(N_EXPERTS, 1),
            w_gate_up=w_gate_up, b_gate_up=b_gate_up, w_down=w_down, b_down=b_down)
        mod = mod_all[l].reshape(MOD_ROWS, 1, N_MOD * D_MODEL)

        p = _premix(x, tile_mod, mod, lw["norm_mix"], w_in[l].astype(BF16))

        lru_par = jnp.concatenate(
            [lru_conv_w[l], lru_conv_b[l][None], lru_ba[l], lru_bi[l], lru_lambda[l],
             jnp.zeros((5, W_MIX), F32)], axis=0)
        w_lru = jnp.concatenate([_block_diag_pairs(lru_wa[l]), _block_diag_pairs(lru_wi[l])], axis=-1).astype(BF16)
        s0_hgrn = jnp.concatenate([jnp.zeros((bp,) + state_hgrn.shape[2:], F32), state_hgrn[:, l]], axis=0)
        s0_lru = jnp.concatenate([jnp.zeros((bp, 2, W_MIX), F32), state_rglru[:, l]], axis=0)
        o_f, hl_f, o_b, hl_b, s_hgrn, s_lru = _scans(p, tabs, lower_bounds[l], lru_par, w_lru, s0_hgrn, s0_lru)
        new_hgrn.append(s_hgrn[:bp])
        new_lru.append(s_lru[:bp])

        x1, h2, ri, rw, counts = _merge(x, p, (o_f, hl_f, o_b, hl_b), tile_mod, mod, lw, n_ctx_tiles)
        x = _moe(l, h2, ri, rw, counts, x1, tile_mod, mod, norm_final_w.reshape(1, D_MODEL), lw,
                 final=(l == depth - 1))

    y_prompt = x[:n_ctx].reshape(bp, tp, D_MODEL)
    y_sample = x[n_ctx:].reshape(bs, ts, D_MODEL)
    return (y_prompt, y_sample, jnp.stack(new_hgrn, axis=1), jnp.stack(new_lru, axis=1))
```

```python
import functools

import jax
import jax.numpy as jnp
from jax import lax
from jax.experimental import pallas as pl
from jax.experimental.pallas import tpu as pltpu

F32 = jnp.float32
BF16 = jnp.bfloat16
I32 = jnp.int32

D_MODEL = 1024
W_MIX = 512
N_HEADS = 4
D_HEAD = 128
N_IN = 10
N_MOD = 6
EPS = 1e-6
LRU_C = 8.0
N_EXPERTS = 32
TOP_K = 4
D_FF = 1024
SWIGLU_LIMIT = 7.0
SWIGLU_ALPHA = 1.702

TM = 256
CHUNK = 64
SUB = 16
N_SUB = CHUNK // SUB
SUB_SHIFT = SUB.bit_length() - 1
DIAG_GROUP = 4
N_CHUNK = TM // CHUNK
ROWS = 8
LOG2E = 1.4426950408889634
DMA_UNROLL = 8
GRID_W = 64
MOD_ROWS = 8
VMEM_LIMIT = 56 * 1024 * 1024


def _dot(a, b):
    return jnp.dot(a, b, preferred_element_type=F32)


def _dot_nt(a, b):
    return lax.dot_general(a, b, (((1,), (1,)), ((), ())), preferred_element_type=F32)


def _sigmoid(x):
    return 0.5 * jnp.tanh(0.5 * x) + 0.5


def _rms(x, w):
    return x * lax.rsqrt(jnp.mean(x * x, axis=-1, keepdims=True) + EPS) * w


def _split3(x):
    a = x.astype(BF16)
    r = x - a.astype(F32)
    b = r.astype(BF16)
    c = (r - b.astype(F32)).astype(BF16)
    return a, b, c


def _const_spec(shape):
    n = len(shape)
    return pl.BlockSpec(shape, lambda *_: (0,) * n, pipeline_mode=pl.Buffered(1))


def _mod_kernel(c_ref, w_ref, b_ref, o_ref):
    c = c_ref[...]
    s = (c * _sigmoid(c)).astype(BF16)
    o_ref[0] = _dot(s, w_ref[0].astype(BF16)) + b_ref[0]


def _modulation(cc, w_mod, b_mod):
    depth, _, width = w_mod.shape
    nb = 1536
    return pl.pallas_call(
        _mod_kernel,
        grid=(depth, width // nb),
        in_specs=[
            pl.BlockSpec((MOD_ROWS, D_MODEL), lambda l, j: (0, 0)),
            pl.BlockSpec((1, D_MODEL, nb), lambda l, j: (l, 0, j)),
            pl.BlockSpec((1, 1, nb), lambda l, j: (l, 0, j)),
        ],
        out_specs=pl.BlockSpec((1, MOD_ROWS, nb), lambda l, j: (l, 0, j)),
        out_shape=jax.ShapeDtypeStruct((depth, MOD_ROWS, width), F32),
        compiler_params=pltpu.CompilerParams(
            dimension_semantics=("arbitrary", "arbitrary"), vmem_limit_bytes=VMEM_LIMIT),
        name="modulation",
    )(cc, w_mod, b_mod.reshape(depth, 1, width))


def _premix_kernel(tmod_ref, x_ref, mod_ref, nw_ref, win_ref, p_ref):
    del tmod_ref
    mod = mod_ref[0]
    h = _rms(x_ref[...], nw_ref[...]) * (1.0 + mod[:, D_MODEL:2 * D_MODEL]) + mod[:, 0:D_MODEL]
    p_ref[...] = _dot(h.astype(BF16), win_ref[...])


def _premix(x, tile_mod, mod, norm_w, w_in_bf):
    n = x.shape[0]
    width = w_in_bf.shape[1]
    grid_spec = pltpu.PrefetchScalarGridSpec(
        num_scalar_prefetch=1,
        grid=(n // TM,),
        in_specs=[
            pl.BlockSpec((TM, D_MODEL), lambda i, tm: (i, 0)),
            pl.BlockSpec((1, 1, N_MOD * D_MODEL), lambda i, tm: (tm[i], 0, 0)),
            _const_spec((1, D_MODEL)),
            _const_spec((D_MODEL, width)),
        ],
        out_specs=pl.BlockSpec((TM, width), lambda i, tm: (i, 0)),
    )
    return pl.pallas_call(
        _premix_kernel,
        grid_spec=grid_spec,
        out_shape=jax.ShapeDtypeStruct((n, width), F32),
        compiler_params=pltpu.CompilerParams(
            dimension_semantics=("arbitrary",), vmem_limit_bytes=VMEM_LIMIT),
        name="premix",
    )(tile_mod, x, mod, norm_w, w_in_bf)


def _shift_rows(x, s):
    return pltpu.roll(x, s % x.shape[0], axis=0)


def _lru_scan(a, b, h0, reverse):
    n, width = a.shape
    nblk = n // ROWS
    a = a.reshape(nblk, ROWS, width)
    b = b.reshape(nblk, ROWS, width)
    row = lax.broadcasted_iota(I32, a.shape, 1)
    s = 1
    while s < ROWS:
        keep = (row < ROWS - s) if reverse else (row >= s)
        shift = (ROWS - s) if reverse else s
        a_sh = jnp.where(keep, pltpu.roll(a, shift, axis=1), 1.0)
        b_sh = jnp.where(keep, pltpu.roll(b, shift, axis=1), 0.0)
        b = a * b_sh + b
        a = a * a_sh
        s *= 2
    edge = 0 if reverse else ROWS - 1
    out = [None] * nblk
    carry = jnp.broadcast_to(h0, (ROWS, width))
    for j in (reversed(range(nblk)) if reverse else range(nblk)):
        hj = a[j] * carry + b[j]
        out[j] = hj
        carry = jnp.broadcast_to(hj[edge:edge + 1, :], (ROWS, width))
    return jnp.concatenate(out, axis=0)


def _pair_blocks(x):
    zero = jnp.zeros((x.shape[0], D_HEAD), x.dtype)
    return jnp.concatenate([jnp.concatenate([x[:, :D_HEAD], zero], axis=1),
                            jnp.concatenate([zero, x[:, D_HEAD:]], axis=1)], axis=0)


def _scan_kernel(tf_ref, tb_ref, first_ref, last_ref, sb_ref, lm1_ref,
                 qf_ref, ff_ref, vf_ref, rxf_ref, qb_ref, fb_ref, vb_ref, rxb_ref,
                 lb_ref, lp_ref, wlru_ref, s0h_ref, s0l_ref,
                 of_ref, hlf_ref, ob_ref, hlb_ref, sfh_ref, sfl_ref,
                 st_ref, hcar_ref, qs_ref, ks_ref, gs_ref, lcs_ref):
    del tf_ref, tb_ref, sb_ref
    step = pl.program_id(0)

    @pl.when(first_ref[step] == 1)
    def _init():
        for d in range(2):
            for h in range(N_HEADS):
                st_ref[d, h] = s0h_ref[0, d, h].T
        hcar_ref[0:2, :] = s0l_ref[0]

    lm1 = lm1_ref[step]
    row = lax.broadcasted_iota(I32, (TM, W_MIX), 0)
    pos = row & lm1

    dirs = ((qf_ref, ff_ref, rxf_ref, hlf_ref), (qb_ref, fb_ref, rxb_ref, hlb_ref))
    for d, (q_ref, f_ref, rx_ref, hl_ref) in enumerate(dirs):
        q = q_ref[...]
        qs_ref[d] = q * _sigmoid(q) * (D_HEAD ** -0.5)
        f = f_ref[...]
        e = jnp.exp(-jnp.abs(f))
        r = 1.0 / (1.0 + e)
        sig_neg = jnp.where(f >= 0, e * r, r)
        log_sig = jnp.minimum(f, 0.0) + jnp.log(r)
        lb = lb_ref[d:d + 1, :]
        ks_ref[d] = (1.0 - lb) * sig_neg
        la = jnp.log(lb)
        lbb = jnp.log1p(-lb) + log_sig
        gs_ref[d] = jnp.maximum(la, lbb) + jnp.log(1.0 + jnp.exp(-jnp.abs(la - lbb)))

        u = rx_ref[...]
        xr = (lp_ref[2:3, :] * u + lp_ref[4:5, :]
              + lp_ref[0:1, :] * jnp.where(pos >= 2, _shift_rows(u, 2), 0.0)
              + lp_ref[1:2, :] * jnp.where(pos >= 1, _shift_rows(u, 1), 0.0)
              + lp_ref[3:4, :] * jnp.where(pos < lm1, _shift_rows(u, -1), 0.0))
        rl, il = [], []
        for nblk in range(W_MIX // 128):
            z = _dot(xr[:, 128 * nblk:128 * (nblk + 1)].astype(BF16), wlru_ref[d, nblk])
            rl.append(z[:, :128])
            il.append(z[:, 128:])
        r_gate = _sigmoid(jnp.concatenate(rl, axis=1) + lp_ref[5 + d:6 + d, :])
        i_gate = _sigmoid(jnp.concatenate(il, axis=1) + lp_ref[7 + d:8 + d, :])
        lam = lp_ref[9 + d:10 + d, :]
        softplus = jnp.maximum(-lam, 0.0) + jnp.log1p(jnp.exp(-jnp.abs(lam)))
        log_a = (-LRU_C) * r_gate * softplus
        a = jnp.exp(log_a)
        th = jnp.tanh(log_a)
        bx = jnp.sqrt(-2.0 * th / (1.0 - th)) * i_gate * xr
        hd = _lru_scan(a, bx, hcar_ref[d:d + 1, :], reverse=(d == 1))
        hl_ref[...] = hd
        edge = TM - 1 if d == 0 else 0
        hcar_ref[d:d + 1, :] = hd[edge:edge + 1, :]

    r64 = lax.broadcasted_iota(I32, (CHUNK, CHUNK), 0)
    c64 = lax.broadcasted_iota(I32, (CHUNK, CHUNK), 1)
    same_sub = (r64 >> SUB_SHIFT) == (c64 >> SUB_SHIFT)
    hrow = lax.broadcasted_iota(I32, (CHUNK // 2, 2 * CHUNK), 0)
    hcol = lax.broadcasted_iota(I32, (CHUNK // 2, 2 * CHUNK), 1) & (CHUNK - 1)
    hrel = hcol - ((hrow >> 3) << SUB_SHIFT)
    hmod = hrow & (ROWS - 1)
    v_refs = (vf_ref, vb_ref)
    o_refs = (of_ref, ob_ref)

    def chunk_body(ci, carry):
        for d in range(2):
            rev = d == 1
            c0 = pl.multiple_of((N_CHUNK - 1 - ci) * CHUNK if rev else ci * CHUNK, CHUNK)
            order = tuple(reversed(range(N_SUB))) if rev else tuple(range(N_SUB))
            q = qs_ref[d, pl.ds(c0, CHUNK), :]
            k = ks_ref[d, pl.ds(c0, CHUNK), :]
            g = gs_ref[d, pl.ds(c0, CHUNK), :]
            v = v_refs[d][pl.ds(c0, CHUNK), :]

            tri = (same_sub & ((c64 >= r64) if rev else (c64 <= r64))).astype(BF16)
            res = _dot(tri, jnp.concatenate(_split3(g), axis=1))
            lc = res[:, 0:W_MIX] + res[:, W_MIX:2 * W_MIX] + res[:, 2 * W_MIX:3 * W_MIX]
            lc2 = lc * LOG2E
            lcs_ref[d] = lc2
            last = 0 if rev else SUB - 1
            tot_sub = [lc[SUB * a + last:SUB * a + last + 1, :] for a in range(N_SUB)]
            totb = jnp.concatenate([jnp.broadcast_to(t, (SUB, W_MIX)) for t in tot_sub], axis=0)

            qh = q * jnp.exp2(lc2)
            kt = k * jnp.exp(totb - lc)

            tot = [tot_sub[a] for a in order]
            cum = [tot[0]]
            for i in range(1, N_SUB):
                cum.append(cum[-1] + tot[i])

            def sub_rows(x, a):
                return x[SUB * a:SUB * (a + 1), :]

            def by_time(fn):
                rows = [None] * N_SUB
                for i, a in enumerate(order):
                    rows[a] = fn(i, a)
                return jnp.concatenate(rows, axis=0)

            zeros_sub = jnp.zeros((SUB, W_MIX), F32)
            qe = by_time(lambda i, a: sub_rows(qh, a) if i == 0 else sub_rows(qh, a) * jnp.exp(cum[i - 1]))
            kl = by_time(lambda j, a: sub_rows(kt, a) if j == N_SUB - 1
                         else sub_rows(kt, a) * jnp.exp(cum[-1] - cum[j]))
            chunk_decay = jnp.exp(cum[-1])
            lhs, kts = [], []
            for j in range(N_SUB - 1):
                def lhs_rows(i, a, j=j):
                    if i <= j:
                        return zeros_sub
                    if i == j + 1:
                        return sub_rows(qh, a)
                    return sub_rows(qh, a) * jnp.exp(cum[i - 1] - cum[j])
                lhs.append(by_time(lhs_rows).astype(BF16))
                kts.append(by_time(lambda i, a, j=j: sub_rows(kt, a) if i == j else zeros_sub).astype(BF16))

            kb = k.astype(BF16)
            vb = v.astype(BF16)
            zero_state = jnp.zeros((D_HEAD, D_HEAD), F32)
            outs = []
            for hp in range(N_HEADS // 2):
                sl = slice(2 * D_HEAD * hp, 2 * D_HEAD * (hp + 1))
                k_pair = _pair_blocks(kb[:, sl])
                att_half = [jnp.zeros((CHUNK // 2, 2 * CHUNK), F32) for _ in range(2)]
                for s0 in range(0, SUB, DIAG_GROUP):
                    pieces, spans = [], []
                    for s in range(s0, s0 + DIAG_GROUP):
                        use = ((True, s >= ROWS) if rev else (s < ROWS, True))
                        for half in range(2):
                            if not use[half]:
                                continue
                            spans.append((s, half, len(pieces) * ROWS))
                            for a in range(N_SUB):
                                r0 = SUB * a + ROWS * half
                                lsel = jnp.broadcast_to(lcs_ref[d, SUB * a + s:SUB * a + s + 1, sl],
                                                        (ROWS, 2 * D_HEAD))
                                pieces.append(q[r0:r0 + ROWS, sl] * jnp.exp2(lc2[r0:r0 + ROWS, sl] - lsel))
                    col = _dot_nt(jnp.concatenate(pieces, axis=0).astype(BF16), k_pair)
                    for s, half, off in spans:
                        rm = hmod + ROWS * half
                        dmask = (hrel == s) & ((rm <= s) if rev else (rm >= s))
                        att_half[half] = jnp.where(dmask, col[off:off + CHUNK // 2], att_half[half])
                att = jnp.concatenate(
                    [att_half[half][ROWS * a:ROWS * (a + 1)] for a in range(N_SUB) for half in range(2)], axis=0)
                for j in range(N_SUB - 1):
                    att = att + _dot_nt(lhs[j][:, sl], _pair_blocks(kts[j][:, sl]))
                st = [st_ref[d, 2 * hp], st_ref[d, 2 * hp + 1]]
                st_pair = jnp.concatenate([jnp.concatenate([st[0], zero_state], axis=1),
                                           jnp.concatenate([zero_state, st[1]], axis=1)], axis=0)
                outs.append(_dot_nt(qe[:, sl].astype(BF16), st_pair.astype(BF16))
                            + _dot(att.astype(BF16), _pair_blocks(vb[:, sl])))
                for i in range(2):
                    hl = slice(D_HEAD * (2 * hp + i), D_HEAD * (2 * hp + i + 1))
                    st_ref[d, 2 * hp + i] = (st[i] * chunk_decay[:, hl]
                                             + _dot(v[:, hl].T.astype(BF16), kl[:, hl].astype(BF16)))
            o_refs[d][pl.ds(c0, CHUNK), :] = jnp.concatenate(outs, axis=1)
        return carry

    lax.fori_loop(0, N_CHUNK, chunk_body, 0)

    @pl.when(last_ref[step] == 1)
    def _final():
        for d in range(2):
            for h in range(N_HEADS):
                sfh_ref[0, d, h] = st_ref[d, h].T
        sfl_ref[0] = hcar_ref[0:2, :]


def _scans(p, tabs, lb, lru_par, w_lru, s0_hgrn, s0_lru):
    n = p.shape[0]
    n_steps = tabs[0].shape[0]
    n_seq = s0_hgrn.shape[0]

    def col(c, which):
        if which == 0:
            return pl.BlockSpec((TM, W_MIX), lambda s, tf, tb, fi, la, sb, lm: (tf[s], c))
        return pl.BlockSpec((TM, W_MIX), lambda s, tf, tb, fi, la, sb, lm: (tb[s], c))

    def out_col(which):
        if which == 0:
            return pl.BlockSpec((TM, W_MIX), lambda s, tf, tb, fi, la, sb, lm: (tf[s], 0))
        return pl.BlockSpec((TM, W_MIX), lambda s, tf, tb, fi, la, sb, lm: (tb[s], 0))

    st_spec = pl.BlockSpec((1, 2, N_HEADS, D_HEAD, D_HEAD),
                           lambda s, tf, tb, fi, la, sb, lm: (sb[s], 0, 0, 0, 0))
    sl_spec = pl.BlockSpec((1, 2, W_MIX), lambda s, tf, tb, fi, la, sb, lm: (sb[s], 0, 0))
    grid_spec = pltpu.PrefetchScalarGridSpec(
        num_scalar_prefetch=6,
        grid=(n_steps,),
        in_specs=[col(0, 0), col(1, 0), col(3, 0), col(8, 0),
                  col(0, 1), col(2, 1), col(3, 1), col(8, 1),
                  _const_spec((2, W_MIX)), _const_spec((16, W_MIX)),
                  _const_spec((2, W_MIX // 128, 128, 256)),
                  st_spec, sl_spec],
        out_specs=[out_col(0), out_col(0), out_col(1), out_col(1), st_spec, sl_spec],
        scratch_shapes=[
            pltpu.VMEM((2, N_HEADS, D_HEAD, D_HEAD), F32),
            pltpu.VMEM((8, W_MIX), F32),
            pltpu.VMEM((2, TM, W_MIX), F32),
            pltpu.VMEM((2, TM, W_MIX), F32),
            pltpu.VMEM((2, TM, W_MIX), F32),
            pltpu.VMEM((2, CHUNK, W_MIX), F32),
        ],
    )
    row = jax.ShapeDtypeStruct((n, W_MIX), F32)
    return pl.pallas_call(
        _scan_kernel,
        grid_spec=grid_spec,
        out_shape=[row, row, row, row,
                   jax.ShapeDtypeStruct((n_seq, 2, N_HEADS, D_HEAD, D_HEAD), F32),
                   jax.ShapeDtypeStruct((n_seq, 2, W_MIX), F32)],
        compiler_params=pltpu.CompilerParams(
            dimension_semantics=("arbitrary",), vmem_limit_bytes=VMEM_LIMIT),
        name="scans",
    )(*tabs, p, p, p, p, p, p, p, p, lb, lru_par, w_lru, s0_hgrn, s0_lru)


def _merge_kernel(n_ctx_tiles, tmod_ref,
                  x_ref, go_ref, sb_ref, sc_ref, sx_ref, rg_ref, of_ref, ob_ref, hlf_ref, hlb_ref,
                  mod_ref, nmix_ref, wmerge_ref, bmerge_ref, wbr_ref, wout_ref, hnorm_ref, scw_ref,
                  nffn_ref, wrt_ref, br_ref,
                  x1_ref, h2_ref, ri_ref, rw_ref, cnt_ref, base_ref):
    del tmod_ref
    i = pl.program_id(0)

    @pl.when(i == 0)
    def _init():
        base_ref[...] = jnp.zeros_like(base_ref)

    mod = mod_ref[0]
    x = x_ref[...]
    h = _rms(x, nmix_ref[...]) * (1.0 + mod[:, D_MODEL:2 * D_MODEL]) + mod[:, 0:D_MODEL]
    gates = _sigmoid(_dot(h.astype(BF16), wmerge_ref[...]) + bmerge_ref[...])

    o = of_ref[...] + ob_ref[...]
    go = go_ref[...]
    parts = []
    for hh in range(N_HEADS):
        sl = slice(D_HEAD * hh, D_HEAD * (hh + 1))
        parts.append(_rms(o[:, sl], hnorm_ref[:, sl]))
    ya = jnp.concatenate(parts, axis=1) * (go * _sigmoid(go))

    lm1 = jnp.where(i < n_ctx_tiles, TM - 1, GRID_W - 1)
    pos = lax.broadcasted_iota(I32, (TM, W_MIX), 0) & lm1
    u = sc_ref[...] * sx_ref[...]
    conv = (scw_ref[1:2, :] * u
            + scw_ref[0:1, :] * jnp.where(pos >= 1, _shift_rows(u, 1), 0.0)
            + scw_ref[2:3, :] * jnp.where(pos < lm1, _shift_rows(u, -1), 0.0))
    yb = sb_ref[...] * conv

    yc = (hlf_ref[...] + hlb_ref[...]) * jax.nn.gelu(rg_ref[...])

    merged = (gates[:, 0:D_MODEL] * _dot(ya.astype(BF16), wbr_ref[0])
              + gates[:, D_MODEL:2 * D_MODEL] * _dot(yb.astype(BF16), wbr_ref[1])
              + gates[:, 2 * D_MODEL:3 * D_MODEL] * _dot(yc.astype(BF16), wbr_ref[2]))
    y = _dot(merged.astype(BF16), wout_ref[...])
    x1 = x + mod[:, 2 * D_MODEL:3 * D_MODEL] * y
    x1_ref[...] = x1

    h2 = _rms(x1, nffn_ref[...]) * (1.0 + mod[:, 4 * D_MODEL:5 * D_MODEL]) + mod[:, 3 * D_MODEL:4 * D_MODEL]
    h2_ref[...] = h2

    h_hi = h2.astype(BF16)
    h_lo = (h2 - h_hi.astype(F32)).astype(BF16)
    wr = wrt_ref[...]
    w_hi = wr.astype(BF16)
    w_lo = (wr - w_hi.astype(F32)).astype(BF16)
    logits = _dot_nt(w_hi, h_hi) + _dot_nt(w_hi, h_lo) + _dot_nt(w_lo, h_hi) + br_ref[...]

    erow = lax.broadcasted_iota(I32, (N_EXPERTS, TM), 0).astype(F32)
    r8 = lax.broadcasted_iota(I32, (8, TM), 0)
    tri = (lax.broadcasted_iota(I32, (TM, TM), 0) <= lax.broadcasted_iota(I32, (TM, TM), 1)).astype(BF16)
    base = base_ref[...]
    top_v, hots, idxs = [], [], []
    work = logits
    for _ in range(TOP_K):
        m = jnp.max(work, axis=0, keepdims=True)
        idx = jnp.min(jnp.where(work == m, erow, float(N_EXPERTS)), axis=0, keepdims=True)
        hot = erow == idx
        work = jnp.where(hot, -jnp.inf, work)
        top_v.append(m)
        hots.append(hot)
        idxs.append(idx)
    sel = hots[0] | hots[1] | hots[2] | hots[3]
    selb = sel.astype(BF16)
    incl = _dot(selb, tri)
    total = _dot(selb, jnp.ones((TM, TM), BF16))
    rank_all = base + incl - 1.0
    ex = [jnp.exp(v - top_v[0]) for v in top_v]
    denom = ex[0] + ex[1] + ex[2] + ex[3]
    ri = jnp.zeros((8, TM), F32)
    rw = jnp.zeros((8, TM), F32)
    for k in range(TOP_K):
        rank_k = jnp.sum(jnp.where(hots[k], rank_all, 0.0), axis=0, keepdims=True)
        ri = jnp.where(r8 == k, idxs[k], ri)
        ri = jnp.where(r8 == TOP_K + k, rank_k, ri)
        rw = jnp.where(r8 == k, ex[k] / denom, rw)
    ri_ref[...] = ri.astype(I32)
    rw_ref[...] = rw
    base = base + total
    base_ref[...] = base
    cnt_ref[...] = base[:, 0:128]


def _merge(x, p, scan_out, tile_mod, mod, lw, n_ctx_tiles):
    n = x.shape[0]
    o_f, hl_f, o_b, hl_b = scan_out

    def tok(width):
        return pl.BlockSpec((TM, width), lambda i, tm: (i, 0))

    def pcol(c):
        return pl.BlockSpec((TM, W_MIX), lambda i, tm: (i, c))

    grid_spec = pltpu.PrefetchScalarGridSpec(
        num_scalar_prefetch=1,
        grid=(n // TM,),
        in_specs=[tok(D_MODEL), pcol(4), pcol(5), pcol(6), pcol(7), pcol(9),
                  tok(W_MIX), tok(W_MIX), tok(W_MIX), tok(W_MIX),
                  pl.BlockSpec((1, 1, N_MOD * D_MODEL), lambda i, tm: (tm[i], 0, 0)),
                  _const_spec((1, D_MODEL)),
                  _const_spec((D_MODEL, 3 * D_MODEL)), _const_spec((1, 3 * D_MODEL)),
                  _const_spec((3, W_MIX, D_MODEL)), _const_spec((D_MODEL, D_MODEL)),
                  _const_spec((1, W_MIX)), _const_spec((3, W_MIX)),
                  _const_spec((1, D_MODEL)), _const_spec((N_EXPERTS, D_MODEL)),
                  _const_spec((N_EXPERTS, 1))],
        out_specs=[tok(D_MODEL), tok(D_MODEL),
                   pl.BlockSpec((8, TM), lambda i, tm: (0, i)),
                   pl.BlockSpec((8, TM), lambda i, tm: (0, i)),
                   pl.BlockSpec((N_EXPERTS, 128), lambda i, tm: (0, 0))],
        scratch_shapes=[pltpu.VMEM((N_EXPERTS, TM), F32)],
    )
    return pl.pallas_call(
        functools.partial(_merge_kernel, n_ctx_tiles),
        grid_spec=grid_spec,
        out_shape=[jax.ShapeDtypeStruct((n, D_MODEL), F32), jax.ShapeDtypeStruct((n, D_MODEL), F32),
                   jax.ShapeDtypeStruct((8, n), I32), jax.ShapeDtypeStruct((8, n), F32),
                   jax.ShapeDtypeStruct((N_EXPERTS, 128), F32)],
        compiler_params=pltpu.CompilerParams(
            dimension_semantics=("arbitrary",), vmem_limit_bytes=VMEM_LIMIT),
        name="merge_router",
    )(tile_mod, x, p, p, p, p, p, o_f, o_b, hl_f, hl_b, mod,
      lw["norm_mix"], lw["w_merge"], lw["b_merge"], lw["w_branch"], lw["w_out"],
      lw["hgrn_norm"], lw["sconv_w"], lw["norm_ffn"], lw["w_router_t"], lw["b_router"])


def _row_copy(src, src_row, dst, dst_row, sem):
    return pltpu.make_async_copy(src.at[pl.ds(src_row, 1)], dst.at[pl.ds(dst_row, 1)], sem)


def _dispatch_kernel(ztile_ref, slot_ref, h2_ref, xs_ref, zeros_ref, sem):
    @pl.when(pl.program_id(0) == 0)
    def _zero_group_tails():
        zeros_ref[...] = jnp.zeros_like(zeros_ref)
        for e in range(2 * N_EXPERTS):
            @pl.when(ztile_ref[e] >= 0)
            def _start(e=e):
                start = pl.multiple_of(ztile_ref[e], TM)
                pltpu.make_async_copy(zeros_ref, xs_ref.at[pl.ds(start, TM)], sem).start()
        for e in range(2 * N_EXPERTS):
            @pl.when(ztile_ref[e] >= 0)
            def _wait():
                pltpu.make_async_copy(zeros_ref, xs_ref.at[pl.ds(0, TM)], sem).wait()

    def issue(r, carry):
        for k in range(TOP_K):
            _row_copy(h2_ref, r, xs_ref, slot_ref[0, 0, k * TM + r], sem).start(priority=k % 2)
        return carry

    lax.fori_loop(0, TM, issue, 0, unroll=DMA_UNROLL)

    def drain(r, carry):
        for k in range(TOP_K):
            _row_copy(h2_ref, 0, xs_ref, 0, sem).wait()
        return carry

    lax.fori_loop(0, TM, drain, 0, unroll=DMA_UNROLL)


def _dispatch(zero_tiles, slots, h2, n_slots):
    n = h2.shape[0]
    grid_spec = pltpu.PrefetchScalarGridSpec(
        num_scalar_prefetch=1,
        grid=(n // TM,),
        in_specs=[pl.BlockSpec((1, 1, TOP_K * TM), lambda i, zt: (i, 0, 0), memory_space=pltpu.SMEM),
                  pl.BlockSpec((TM, D_MODEL), lambda i, zt: (i, 0))],
        out_specs=pl.BlockSpec(memory_space=pl.ANY),
        scratch_shapes=[pltpu.VMEM((TM, D_MODEL), F32), pltpu.SemaphoreType.DMA],
    )
    return pl.pallas_call(
        _dispatch_kernel,
        grid_spec=grid_spec,
        out_shape=jax.ShapeDtypeStruct((n_slots, D_MODEL), F32),
        compiler_params=pltpu.CompilerParams(dimension_semantics=("arbitrary",)),
        name="moe_dispatch",
    )(zero_tiles, slots, h2)


def _ffn_kernel(layer, te_ref, nused_ref, first_ref, last_ref, nxt_ref, par_ref,
                xs_ref, wgu_hbm, bgu_ref, wd_hbm, bd_ref, ys_ref,
                stage_gu, stage_d, wgu_bf, wd_bf, sem):
    i = pl.program_id(0)
    used = i < nused_ref[0]

    def weight_copies(e):
        return (pltpu.make_async_copy(wgu_hbm.at[layer, e], stage_gu, sem.at[0]),
                pltpu.make_async_copy(wd_hbm.at[layer, e], stage_d, sem.at[1]))

    def cast_into(slot):
        wgu_bf[slot] = stage_gu[...].astype(BF16)
        wd_bf[slot] = stage_d[...].astype(BF16)

    @pl.when(used)
    def _compute():
        par = par_ref[i]

        @pl.when(i == 0)
        def _load_first():
            for c in weight_copies(te_ref[0]):
                c.start()
            for c in weight_copies(te_ref[0]):
                c.wait()
            cast_into(par)

        has_next = nxt_ref[i] >= 0

        @pl.when((first_ref[i] == 1) & has_next)
        def _prefetch_next():
            for c in weight_copies(nxt_ref[i]):
                c.start()

        gu = _dot(xs_ref[...].astype(BF16), wgu_bf[par]) + bgu_ref[0, 0]
        glu = jnp.minimum(gu[:, 0:D_FF], SWIGLU_LIMIT)
        lin = jnp.clip(gu[:, D_FF:2 * D_FF], -SWIGLU_LIMIT, SWIGLU_LIMIT)
        act = glu * _sigmoid(SWIGLU_ALPHA * glu) * (lin + 1.0)
        ys_ref[...] = _dot(act.astype(BF16), wd_bf[par]) + bd_ref[0, 0]

        @pl.when((last_ref[i] == 1) & has_next)
        def _stage_next():
            for c in weight_copies(nxt_ref[i]):
                c.wait()
            cast_into(1 - par)

    @pl.when(jnp.logical_not(used))
    def _skip():
        ys_ref[...] = jnp.zeros_like(ys_ref)


def _ffn(layer, tabs, xs, w_gate_up, b_gate_up, w_down, b_down):
    rows = xs.shape[0]
    depth = w_gate_up.shape[0]
    grid_spec = pltpu.PrefetchScalarGridSpec(
        num_scalar_prefetch=6,
        grid=(rows // TM,),
        in_specs=[
            pl.BlockSpec((TM, D_MODEL), lambda i, te, nu, *_: (jnp.minimum(i, nu[0] - 1), 0)),
            pl.BlockSpec(memory_space=pl.ANY),
            pl.BlockSpec((1, 1, 1, 2 * D_FF), lambda i, te, nu, *_: (layer, te[i], 0, 0)),
            pl.BlockSpec(memory_space=pl.ANY),
            pl.BlockSpec((1, 1, 1, D_MODEL), lambda i, te, nu, *_: (layer, te[i], 0, 0)),
        ],
        out_specs=pl.BlockSpec((TM, D_MODEL), lambda i, te, nu, *_: (i, 0)),
        scratch_shapes=[pltpu.VMEM((D_MODEL, 2 * D_FF), F32), pltpu.VMEM((D_FF, D_MODEL), F32),
                        pltpu.VMEM((2, D_MODEL, 2 * D_FF), BF16), pltpu.VMEM((2, D_FF, D_MODEL), BF16),
                        pltpu.SemaphoreType.DMA((2,))],
    )
    return pl.pallas_call(
        functools.partial(_ffn_kernel, layer),
        grid_spec=grid_spec,
        out_shape=jax.ShapeDtypeStruct((rows, D_MODEL), F32),
        compiler_params=pltpu.CompilerParams(
            dimension_semantics=("arbitrary",), vmem_limit_bytes=VMEM_LIMIT),
        name="moe_ffn",
    )(*tabs, xs, w_gate_up, b_gate_up.reshape(depth, N_EXPERTS, 1, 2 * D_FF),
      w_down, b_down.reshape(depth, N_EXPERTS, 1, D_MODEL))


def _combine_kernel(final, tmod_ref, slot_ref, slot_next_ref, x1_ref, rw_ref, mod_ref, nfin_ref, ys_ref,
                    out_ref, buf, sem):
    del tmod_ref
    i = pl.program_id(0)
    cur = i % 2

    def gather(slots, which):
        def issue(r, carry):
            for k in range(TOP_K):
                pltpu.make_async_copy(ys_ref.at[pl.ds(slots[0, 0, k * TM + r], 1)],
                                      buf.at[which, k, pl.ds(r, 1)], sem.at[which]).start(priority=k % 2)
            return carry
        lax.fori_loop(0, TM, issue, 0, unroll=DMA_UNROLL)

    @pl.when(i == 0)
    def _first():
        gather(slot_ref, 0)

    @pl.when(i + 1 < pl.num_programs(0))
    def _next():
        gather(slot_next_ref, 1 - cur)

    def drain(r, carry):
        for k in range(TOP_K):
            pltpu.make_async_copy(ys_ref.at[pl.ds(0, 1)], buf.at[cur, 0, pl.ds(0, 1)], sem.at[cur]).wait()
        return carry

    lax.fori_loop(0, TM, drain, 0, unroll=DMA_UNROLL)

    rw = rw_ref[...]
    acc = rw[:, 0:1] * buf[cur, 0]
    for k in range(1, TOP_K):
        acc = acc + rw[:, k:k + 1] * buf[cur, k]
    out = x1_ref[...] + mod_ref[0][:, 5 * D_MODEL:6 * D_MODEL] * acc
    if final:
        out = _rms(out, nfin_ref[...])
    out_ref[...] = out


def _combine(tile_mod, slots, x1, rw_t, mod, norm_final, ys, final):
    n = x1.shape[0]
    last = n // TM - 1
    grid_spec = pltpu.PrefetchScalarGridSpec(
        num_scalar_prefetch=1,
        grid=(n // TM,),
        in_specs=[pl.BlockSpec((1, 1, TOP_K * TM), lambda i, tm: (i, 0, 0), memory_space=pltpu.SMEM),
                  pl.BlockSpec((1, 1, TOP_K * TM), lambda i, tm: (jnp.minimum(i + 1, last), 0, 0),
                               memory_space=pltpu.SMEM),
                  pl.BlockSpec((TM, D_MODEL), lambda i, tm: (i, 0)),
                  pl.BlockSpec((TM, 8), lambda i, tm: (i, 0)),
                  pl.BlockSpec((1, 1, N_MOD * D_MODEL), lambda i, tm: (tm[i], 0, 0)),
                  pl.BlockSpec((1, D_MODEL), lambda i, tm: (0, 0)),
                  pl.BlockSpec(memory_space=pl.ANY)],
        out_specs=pl.BlockSpec((TM, D_MODEL), lambda i, tm: (i, 0)),
        scratch_shapes=[pltpu.VMEM((2, TOP_K, TM, D_MODEL), F32), pltpu.SemaphoreType.DMA((2,))],
    )
    return pl.pallas_call(
        functools.partial(_combine_kernel, final),
        grid_spec=grid_spec,
        out_shape=jax.ShapeDtypeStruct((n, D_MODEL), F32),
        compiler_params=pltpu.CompilerParams(
            dimension_semantics=("arbitrary",), vmem_limit_bytes=VMEM_LIMIT),
        name="moe_combine",
    )(tile_mod, slots, slots, x1, rw_t, mod, norm_final, ys)


def _moe(layer, h2, ri, rw, counts, x1, tile_mod, mod, norm_final, lw, final):
    n = h2.shape[0]
    n_tiles = n // TM
    n_slots = n * TOP_K + N_EXPERTS * TM
    cnt = counts[:, 0].astype(I32)
    padded = ((cnt + TM - 1) // TM) * TM
    ends = jnp.cumsum(padded)
    offsets = (ends - padded).astype(I32)
    n_used = (ends[-1] // TM).astype(I32)
    tile_ids = jnp.minimum(jnp.arange(n_slots // TM, dtype=I32), n_used - 1)
    tile_expert = jnp.sum((ends[None, :] // TM <= tile_ids[:, None]).astype(I32), axis=1)
    onehot = ri[0:TOP_K, :, None] == jnp.arange(N_EXPERTS, dtype=I32)
    slot = jnp.sum(jnp.where(onehot, offsets, 0), axis=-1) + ri[TOP_K:2 * TOP_K]
    slots = slot.reshape(TOP_K, n_tiles, TM).transpose(1, 0, 2).reshape(n_tiles, 1, TOP_K * TM)
    tail = n_used + jnp.arange(N_EXPERTS, dtype=I32)
    zero_tiles = jnp.concatenate([jnp.where(padded > 0, ends - TM, -1),
                                  jnp.where(tail < n_slots // TM, tail * TM, -1)]).astype(I32)

    prev_e = jnp.concatenate([jnp.full((1,), -1, I32), tile_expert[:-1]])
    first = (tile_expert != prev_e).astype(I32)
    next_tile = (ends // TM)[tile_expert]
    last = (tile_ids + 1 == next_tile).astype(I32)
    nxt = jnp.where(next_tile < n_used, tile_expert[jnp.minimum(next_tile, n_slots // TM - 1)], -1).astype(I32)
    parity = ((jnp.cumsum(first) - 1) & 1).astype(I32)
    ffn_tabs = (tile_expert, n_used.reshape(1), first, last, nxt, parity)

    xs = _dispatch(zero_tiles, slots, h2, n_slots)
    ys = _ffn(layer, ffn_tabs, xs, lw["w_gate_up"], lw["b_gate_up"], lw["w_down"], lw["b_down"])
    return _combine(tile_mod, slots, x1, rw.T, mod, norm_final, ys, final)


def _block_diag_pairs(w):
    z = jnp.zeros_like(w[:, 0::2])
    top = jnp.concatenate([w[:, 0::2], z], axis=-1)
    bot = jnp.concatenate([z, w[:, 1::2]], axis=-1)
    return jnp.concatenate([top, bot], axis=-2)


def kernel(x_prompt, x_sample, state_hgrn, state_rglru, c, c_ctx, norm_mix_w, norm_ffn_w, norm_final_w, w_mod, b_mod, w_in, hgrn_lb_logits, hgrn_norm_w, sconv_w, lru_conv_w, lru_conv_b, lru_wa, lru_ba, lru_wi, lru_bi, lru_lambda, w_branch, w_merge, b_merge, w_out, w_router, b_router, w_gate_up, b_gate_up, w_down, b_down):
    depth = w_in.shape[0]
    bp, tp, _ = x_prompt.shape
    bs, ts, _ = x_sample.shape
    assert tp == TM and ts % TM == 0 and bs + 1 <= MOD_ROWS
    n_ctx = bp * tp
    n_ctx_tiles = n_ctx // TM
    nt_lat = ts // TM

    x = jnp.concatenate([x_prompt.reshape(n_ctx, D_MODEL), x_sample.reshape(bs * ts, D_MODEL)], axis=0)
    n_tiles = x.shape[0] // TM

    lb_cum = jnp.cumsum(jax.nn.softmax(hgrn_lb_logits.astype(F32), axis=0), axis=0)
    lower_bounds = lb_cum - lb_cum[0:1]

    cc = jnp.zeros((MOD_ROWS, D_MODEL), F32).at[0].set(c_ctx).at[1:1 + bs].set(c)
    mod_all = _modulation(cc, w_mod, b_mod)
    tile_mod = jnp.concatenate([jnp.zeros((n_ctx_tiles,), I32),
                                1 + jnp.arange(bs * nt_lat, dtype=I32) // nt_lat])

    lat = jnp.arange(bs * nt_lat, dtype=I32)
    lat_b, lat_c = lat // nt_lat, lat % nt_lat
    ctx = jnp.arange(bp, dtype=I32)
    one = jnp.ones_like(ctx)
    tabs = (
        jnp.concatenate([ctx, n_ctx_tiles + lat]),
        jnp.concatenate([ctx, n_ctx_tiles + lat_b * nt_lat + (nt_lat - 1 - lat_c)]),
        jnp.concatenate([one, (lat_c == 0).astype(I32)]),
        jnp.concatenate([one, (lat_c == nt_lat - 1).astype(I32)]),
        jnp.concatenate([ctx, bp + lat_b]),
        jnp.concatenate([one * (TM - 1), jnp.full_like(lat, GRID_W - 1)]),
    )

    new_hgrn, new_lru = [], []
    for l in range(depth):
        lw = dict(
            norm_mix=norm_mix_w[l].reshape(1, D_MODEL), norm_ffn=norm_ffn_w[l].reshape(1, D_MODEL),
            w_merge=w_merge[l].astype(BF16), b_merge=b_merge[l].reshape(1, 3 * D_MODEL),
            w_branch=w_branch[l].astype(BF16), w_out=w_out[l].astype(BF16),
            hgrn_norm=hgrn_norm_w[l].reshape(1, W_MIX), sconv_w=sconv_w[l],
            w_router_t=w_router[l].T, b_router=b_router[l].reshape(N_EXPERTS, 1),
            w_gate_up=w_gate_up, b_gate_up=b_gate_up, w_down=w_down, b_down=b_down)
        mod = mod_all[l].reshape(MOD_ROWS, 1, N_MOD * D_MODEL)

        p = _premix(x, tile_mod, mod, lw["norm_mix"], w_in[l].astype(BF16))

        lru_par = jnp.concatenate(
            [lru_conv_w[l], lru_conv_b[l][None], lru_ba[l], lru_bi[l], lru_lambda[l],
             jnp.zeros((5, W_MIX), F32)], axis=0)
        w_lru = jnp.concatenate([_block_diag_pairs(lru_wa[l]), _block_diag_pairs(lru_wi[l])], axis=-1).astype(BF16)
        s0_hgrn = jnp.concatenate([jnp.zeros((bp,) + state_hgrn.shape[2:], F32), state_hgrn[:, l]], axis=0)
        s0_lru = jnp.concatenate([jnp.zeros((bp, 2, W_MIX), F32), state_rglru[:, l]], axis=0)
        o_f, hl_f, o_b, hl_b, s_hgrn, s_lru = _scans(p, tabs, lower_bounds[l], lru_par, w_lru, s0_hgrn, s0_lru)
        new_hgrn.append(s_hgrn[:bp])
        new_lru.append(s_lru[:bp])

        x1, h2, ri, rw, counts = _merge(x, p, (o_f, hl_f, o_b, hl_b), tile_mod, mod, lw, n_ctx_tiles)
        x = _moe(l, h2, ri, rw, counts, x1, tile_mod, mod, norm_final_w.reshape(1, D_MODEL), lw,
                 final=(l == depth - 1))

    y_prompt = x[:n_ctx].reshape(bp, tp, D_MODEL)
    y_sample = x[n_ctx:].reshape(bs, ts, D_MODEL)
    return (y_prompt, y_sample, jnp.stack(new_hgrn, axis=1), jnp.stack(new_lru, axis=1))
```

```python
import functools

import jax
import jax.numpy as jnp
from jax import lax
from jax.experimental import pallas as pl
from jax.experimental.pallas import tpu as pltpu

F32 = jnp.float32
BF16 = jnp.bfloat16
I32 = jnp.int32

D_MODEL = 1024
W_MIX = 512
N_HEADS = 4
D_HEAD = 128
N_IN = 10
N_MOD = 6
EPS = 1e-6
LRU_C = 8.0
N_EXPERTS = 32
TOP_K = 4
D_FF = 1024
SWIGLU_LIMIT = 7.0
SWIGLU_ALPHA = 1.702

TM = 256
CHUNK = 64
SUB = 16
N_SUB = CHUNK // SUB
SUB_SHIFT = SUB.bit_length() - 1
HALVES = SUB // 8
DIAG_GROUP = 8
N_CHUNK = TM // CHUNK
ROWS = 8
LOG2E = 1.4426950408889634
DMA_UNROLL = 8
GRID_W = 64
MOD_ROWS = 8
VMEM_LIMIT = 56 * 1024 * 1024


def _dot(a, b):
    return jnp.dot(a, b, preferred_element_type=F32)


def _dot_nt(a, b):
    return lax.dot_general(a, b, (((1,), (1,)), ((), ())), preferred_element_type=F32)


def _sigmoid(x):
    return 0.5 * jnp.tanh(0.5 * x) + 0.5


def _rms(x, w):
    return x * lax.rsqrt(jnp.mean(x * x, axis=-1, keepdims=True) + EPS) * w


def _split3(x):
    a = x.astype(BF16)
    r = x - a.astype(F32)
    b = r.astype(BF16)
    c = (r - b.astype(F32)).astype(BF16)
    return a, b, c


def _const_spec(shape):
    n = len(shape)
    return pl.BlockSpec(shape, lambda *_: (0,) * n, pipeline_mode=pl.Buffered(1))


def _mod_kernel(c_ref, w_ref, b_ref, o_ref):
    c = c_ref[...]
    s = (c * _sigmoid(c)).astype(BF16)
    o_ref[0] = _dot(s, w_ref[0].astype(BF16)) + b_ref[0]


def _modulation(cc, w_mod, b_mod):
    depth, _, width = w_mod.shape
    nb = 1536
    return pl.pallas_call(
        _mod_kernel,
        grid=(depth, width // nb),
        in_specs=[
            pl.BlockSpec((MOD_ROWS, D_MODEL), lambda l, j: (0, 0)),
            pl.BlockSpec((1, D_MODEL, nb), lambda l, j: (l, 0, j)),
            pl.BlockSpec((1, 1, nb), lambda l, j: (l, 0, j)),
        ],
        out_specs=pl.BlockSpec((1, MOD_ROWS, nb), lambda l, j: (l, 0, j)),
        out_shape=jax.ShapeDtypeStruct((depth, MOD_ROWS, width), F32),
        compiler_params=pltpu.CompilerParams(
            dimension_semantics=("arbitrary", "arbitrary"), vmem_limit_bytes=VMEM_LIMIT),
        name="modulation",
    )(cc, w_mod, b_mod.reshape(depth, 1, width))


C_V, C_GO, C_SB, C_SC, C_SX, C_RG, C_QS, C_KF, C_GF, C_KB, C_GB, C_XR = range(12)


def _premix_kernel(n_ctx_tiles, tmod_ref, x_ref, mod_ref, nw_ref, win_ref, lb_ref, lp_ref, o_ref):
    del tmod_ref
    i = pl.program_id(0)
    mod = mod_ref[0]
    h = _rms(x_ref[...], nw_ref[...]) * (1.0 + mod[:, D_MODEL:2 * D_MODEL]) + mod[:, 0:D_MODEL]
    p = _dot(h.astype(BF16), win_ref[...])

    def blk(c):
        return p[:, W_MIX * c:W_MIX * (c + 1)]

    def put(c, val):
        o_ref[:, W_MIX * c:W_MIX * (c + 1)] = val

    for c_out, c_in in ((C_V, 3), (C_GO, 4), (C_SB, 5), (C_SC, 6), (C_SX, 7), (C_RG, 9)):
        put(c_out, blk(c_in))

    q = blk(0)
    put(C_QS, q * _sigmoid(q) * (D_HEAD ** -0.5))
    for d, (c_k, c_g) in enumerate(((C_KF, C_GF), (C_KB, C_GB))):
        f = blk(1 + d)
        e = jnp.exp(-jnp.abs(f))
        r = 1.0 / (1.0 + e)
        sig_neg = jnp.where(f >= 0, e * r, r)
        log_sig = jnp.minimum(f, 0.0) + jnp.log(r)
        lb = lb_ref[d:d + 1, :]
        put(c_k, (1.0 - lb) * sig_neg)
        la = jnp.log(lb)
        lbb = jnp.log1p(-lb) + log_sig
        put(c_g, jnp.maximum(la, lbb) + jnp.log(1.0 + jnp.exp(-jnp.abs(la - lbb))))

    lm1 = jnp.where(i < n_ctx_tiles, TM - 1, GRID_W - 1)
    pos = lax.broadcasted_iota(I32, (TM, W_MIX), 0) & lm1
    u = blk(8)
    put(C_XR, lp_ref[2:3, :] * u + lp_ref[4:5, :]
        + lp_ref[0:1, :] * jnp.where(pos >= 2, _shift_rows(u, 2), 0.0)
        + lp_ref[1:2, :] * jnp.where(pos >= 1, _shift_rows(u, 1), 0.0)
        + lp_ref[3:4, :] * jnp.where(pos < lm1, _shift_rows(u, -1), 0.0))


def _premix(x, tile_mod, mod, norm_w, w_in_bf, lb, lru_par, n_ctx_tiles):
    n = x.shape[0]
    width = 12 * W_MIX
    grid_spec = pltpu.PrefetchScalarGridSpec(
        num_scalar_prefetch=1,
        grid=(n // TM,),
        in_specs=[
            pl.BlockSpec((TM, D_MODEL), lambda i, tm: (i, 0)),
            pl.BlockSpec((1, 1, N_MOD * D_MODEL), lambda i, tm: (tm[i], 0, 0)),
            _const_spec((1, D_MODEL)),
            _const_spec(w_in_bf.shape),
            _const_spec((2, W_MIX)),
            _const_spec((16, W_MIX)),
        ],
        out_specs=pl.BlockSpec((TM, width), lambda i, tm: (i, 0)),
    )
    return pl.pallas_call(
        functools.partial(_premix_kernel, n_ctx_tiles),
        grid_spec=grid_spec,
        out_shape=jax.ShapeDtypeStruct((n, width), F32),
        compiler_params=pltpu.CompilerParams(
            dimension_semantics=("arbitrary",), vmem_limit_bytes=VMEM_LIMIT),
        name="premix",
    )(tile_mod, x, mod, norm_w, w_in_bf, lb, lru_par)


def _shift_rows(x, s):
    return pltpu.roll(x, s % x.shape[0], axis=0)


def _lru_scan(a, b, h0, reverse):
    n, width = a.shape
    nblk = n // ROWS
    a = a.reshape(nblk, ROWS, width)
    b = b.reshape(nblk, ROWS, width)
    row = lax.broadcasted_iota(I32, a.shape, 1)
    s = 1
    while s < ROWS:
        keep = (row < ROWS - s) if reverse else (row >= s)
        shift = (ROWS - s) if reverse else s
        a_sh = jnp.where(keep, pltpu.roll(a, shift, axis=1), 1.0)
        b_sh = jnp.where(keep, pltpu.roll(b, shift, axis=1), 0.0)
        b = a * b_sh + b
        a = a * a_sh
        s *= 2
    edge = 0 if reverse else ROWS - 1
    out = [None] * nblk
    carry = jnp.broadcast_to(h0, (ROWS, width))
    for j in (reversed(range(nblk)) if reverse else range(nblk)):
        hj = a[j] * carry + b[j]
        out[j] = hj
        carry = jnp.broadcast_to(hj[edge:edge + 1, :], (ROWS, width))
    return jnp.concatenate(out, axis=0)


def _pair_blocks(x):
    zero = jnp.zeros((x.shape[0], D_HEAD), x.dtype)
    return jnp.concatenate([jnp.concatenate([x[:, :D_HEAD], zero], axis=1),
                            jnp.concatenate([zero, x[:, D_HEAD:]], axis=1)], axis=0)


def _scan_kernel(tf_ref, tb_ref, first_ref, last_ref, sb_ref,
                 qf_ref, kf_ref, gf_ref, vf_ref, xrf_ref, qb_ref, kb_ref, gb_ref, vb_ref, xrb_ref,
                 lp_ref, wlru_ref, s0h_ref, s0l_ref,
                 of_ref, hlf_ref, ob_ref, hlb_ref, sfh_ref, sfl_ref,
                 st_ref, hcar_ref, lcs_ref):
    del tf_ref, tb_ref, sb_ref
    step = pl.program_id(0)

    @pl.when(first_ref[step] == 1)
    def _init():
        for d in range(2):
            for h in range(N_HEADS):
                st_ref[d, h] = s0h_ref[0, d, h].T
        hcar_ref[0:2, :] = s0l_ref[0]

    for d, (xr_ref, hl_ref) in enumerate(((xrf_ref, hlf_ref), (xrb_ref, hlb_ref))):
        xr = xr_ref[...]
        rl, il = [], []
        for nblk in range(W_MIX // 128):
            z = _dot(xr[:, 128 * nblk:128 * (nblk + 1)].astype(BF16), wlru_ref[d, nblk])
            rl.append(z[:, :128])
            il.append(z[:, 128:])
        r_gate = _sigmoid(jnp.concatenate(rl, axis=1) + lp_ref[5 + d:6 + d, :])
        i_gate = _sigmoid(jnp.concatenate(il, axis=1) + lp_ref[7 + d:8 + d, :])
        lam = lp_ref[9 + d:10 + d, :]
        softplus = jnp.maximum(-lam, 0.0) + jnp.log1p(jnp.exp(-jnp.abs(lam)))
        log_a = (-LRU_C) * r_gate * softplus
        a = jnp.exp(log_a)
        th = jnp.tanh(log_a)
        bx = jnp.sqrt(-2.0 * th / (1.0 - th)) * i_gate * xr
        hd = _lru_scan(a, bx, hcar_ref[d:d + 1, :], reverse=(d == 1))
        hl_ref[...] = hd
        edge = TM - 1 if d == 0 else 0
        hcar_ref[d:d + 1, :] = hd[edge:edge + 1, :]

    r64 = lax.broadcasted_iota(I32, (CHUNK, CHUNK), 0)
    c64 = lax.broadcasted_iota(I32, (CHUNK, CHUNK), 1)
    same_sub = (r64 >> SUB_SHIFT) == (c64 >> SUB_SHIFT)
    hrow = lax.broadcasted_iota(I32, (CHUNK // HALVES, 2 * CHUNK), 0)
    hcol = lax.broadcasted_iota(I32, (CHUNK // HALVES, 2 * CHUNK), 1) & (CHUNK - 1)
    hrel = hcol - ((hrow >> 3) << SUB_SHIFT)
    hmod = hrow & (ROWS - 1)
    q_refs, k_refs, g_refs = (qf_ref, qb_ref), (kf_ref, kb_ref), (gf_ref, gb_ref)
    v_refs = (vf_ref, vb_ref)
    o_refs = (of_ref, ob_ref)

    def chunk_body(ci, carry):
        for d in range(2):
            rev = d == 1
            c0 = pl.multiple_of((N_CHUNK - 1 - ci) * CHUNK if rev else ci * CHUNK, CHUNK)
            order = tuple(reversed(range(N_SUB))) if rev else tuple(range(N_SUB))
            q = q_refs[d][pl.ds(c0, CHUNK), :]
            k = k_refs[d][pl.ds(c0, CHUNK), :]
            g = g_refs[d][pl.ds(c0, CHUNK), :]
            v = v_refs[d][pl.ds(c0, CHUNK), :]

            tri = (same_sub & ((c64 >= r64) if rev else (c64 <= r64))).astype(BF16)
            res = _dot(tri, jnp.concatenate(_split3(g), axis=1))
            lc = res[:, 0:W_MIX] + res[:, W_MIX:2 * W_MIX] + res[:, 2 * W_MIX:3 * W_MIX]
            lc2 = lc * LOG2E
            lcs_ref[d] = lc2
            last = 0 if rev else SUB - 1
            tot_sub = [lc[SUB * a + last:SUB * a + last + 1, :] for a in range(N_SUB)]
            totb = jnp.concatenate([jnp.broadcast_to(t, (SUB, W_MIX)) for t in tot_sub], axis=0)

            qh = q * jnp.exp2(lc2)
            kt = k * jnp.exp(totb - lc)

            tot = [tot_sub[a] for a in order]
            cum = [tot[0]]
            for i in range(1, N_SUB):
                cum.append(cum[-1] + tot[i])

            def sub_rows(x, a):
                return x[SUB * a:SUB * (a + 1), :]

            def by_time(fn):
                rows = [None] * N_SUB
                for i, a in enumerate(order):
                    rows[a] = fn(i, a)
                return jnp.concatenate(rows, axis=0)

            zeros_sub = jnp.zeros((SUB, W_MIX), F32)
            qe = by_time(lambda i, a: sub_rows(qh, a) if i == 0 else sub_rows(qh, a) * jnp.exp(cum[i - 1]))
            kl = by_time(lambda j, a: sub_rows(kt, a) if j == N_SUB - 1
                         else sub_rows(kt, a) * jnp.exp(cum[-1] - cum[j]))
            chunk_decay = jnp.exp(cum[-1])
            lhs, kts = [], []
            for j in range(N_SUB - 1):
                def lhs_rows(i, a, j=j):
                    if i <= j:
                        return zeros_sub
                    if i == j + 1:
                        return sub_rows(qh, a)
                    return sub_rows(qh, a) * jnp.exp(cum[i - 1] - cum[j])
                lhs.append(by_time(lhs_rows).astype(BF16))
                kts.append(by_time(lambda i, a, j=j: sub_rows(kt, a) if i == j else zeros_sub).astype(BF16))

            kb = k.astype(BF16)
            vb = v.astype(BF16)
            zero_state = jnp.zeros((D_HEAD, D_HEAD), F32)
            outs = []
            for hp in range(N_HEADS // 2):
                sl = slice(2 * D_HEAD * hp, 2 * D_HEAD * (hp + 1))
                k_pair = _pair_blocks(kb[:, sl])
                att_half = [jnp.zeros((CHUNK // HALVES, 2 * CHUNK), F32) for _ in range(HALVES)]
                for s0 in range(0, SUB, DIAG_GROUP):
                    pieces, spans = [], []
                    for s in range(s0, s0 + DIAG_GROUP):
                        use = ((True,) if HALVES == 1 else (True, s >= ROWS) if rev else (s < ROWS, True))
                        for half in range(HALVES):
                            if not use[half]:
                                continue
                            spans.append((s, half, len(pieces) * ROWS))
                            for a in range(N_SUB):
                                r0 = SUB * a + ROWS * half
                                lsel = jnp.broadcast_to(lcs_ref[d, SUB * a + s:SUB * a + s + 1, sl],
                                                        (ROWS, 2 * D_HEAD))
                                pieces.append(q[r0:r0 + ROWS, sl] * jnp.exp2(lc2[r0:r0 + ROWS, sl] - lsel))
                    col = _dot_nt(jnp.concatenate(pieces, axis=0).astype(BF16), k_pair)
                    for s, half, off in spans:
                        rm = hmod + ROWS * half
                        dmask = (hrel == s) & ((rm <= s) if rev else (rm >= s))
                        att_half[half] = jnp.where(dmask, col[off:off + CHUNK // HALVES], att_half[half])
                att = jnp.concatenate(
                    [att_half[half][ROWS * a:ROWS * (a + 1)] for a in range(N_SUB) for half in range(HALVES)], axis=0)
                for j in range(N_SUB - 1):
                    att = att + _dot_nt(lhs[j][:, sl], _pair_blocks(kts[j][:, sl]))
                st = [st_ref[d, 2 * hp], st_ref[d, 2 * hp + 1]]
                st_pair = jnp.concatenate([jnp.concatenate([st[0], zero_state], axis=1),
                                           jnp.concatenate([zero_state, st[1]], axis=1)], axis=0)
                outs.append(_dot_nt(qe[:, sl].astype(BF16), st_pair.astype(BF16))
                            + _dot(att.astype(BF16), _pair_blocks(vb[:, sl])))
                for i in range(2):
                    hl = slice(D_HEAD * (2 * hp + i), D_HEAD * (2 * hp + i + 1))
                    st_ref[d, 2 * hp + i] = (st[i] * chunk_decay[:, hl]
                                             + _dot(v[:, hl].T.astype(BF16), kl[:, hl].astype(BF16)))
            o_refs[d][pl.ds(c0, CHUNK), :] = jnp.concatenate(outs, axis=1)
        return carry

    lax.fori_loop(0, N_CHUNK, chunk_body, 0)

    @pl.when(last_ref[step] == 1)
    def _final():
        for d in range(2):
            for h in range(N_HEADS):
                sfh_ref[0, d, h] = st_ref[d, h].T
        sfl_ref[0] = hcar_ref[0:2, :]


def _scans(pre, tabs, lru_par, w_lru, s0_hgrn, s0_lru):
    n = pre.shape[0]
    n_steps = tabs[0].shape[0]
    n_seq = s0_hgrn.shape[0]

    def col(c, which):
        if which == 0:
            return pl.BlockSpec((TM, W_MIX), lambda s, tf, tb, fi, la, sb: (tf[s], c))
        return pl.BlockSpec((TM, W_MIX), lambda s, tf, tb, fi, la, sb: (tb[s], c))

    st_spec = pl.BlockSpec((1, 2, N_HEADS, D_HEAD, D_HEAD), lambda s, tf, tb, fi, la, sb: (sb[s], 0, 0, 0, 0))
    sl_spec = pl.BlockSpec((1, 2, W_MIX), lambda s, tf, tb, fi, la, sb: (sb[s], 0, 0))
    grid_spec = pltpu.PrefetchScalarGridSpec(
        num_scalar_prefetch=5,
        grid=(n_steps,),
        in_specs=[col(C_QS, 0), col(C_KF, 0), col(C_GF, 0), col(C_V, 0), col(C_XR, 0),
                  col(C_QS, 1), col(C_KB, 1), col(C_GB, 1), col(C_V, 1), col(C_XR, 1),
                  _const_spec((16, W_MIX)),
                  _const_spec((2, W_MIX // 128, 128, 256)),
                  st_spec, sl_spec],
        out_specs=[col(0, 0), col(0, 0), col(0, 1), col(0, 1), st_spec, sl_spec],
        scratch_shapes=[
            pltpu.VMEM((2, N_HEADS, D_HEAD, D_HEAD), F32),
            pltpu.VMEM((8, W_MIX), F32),
            pltpu.VMEM((2, CHUNK, W_MIX), F32),
        ],
    )
    row = jax.ShapeDtypeStruct((n, W_MIX), F32)
    return pl.pallas_call(
        _scan_kernel,
        grid_spec=grid_spec,
        out_shape=[row, row, row, row,
                   jax.ShapeDtypeStruct((n_seq, 2, N_HEADS, D_HEAD, D_HEAD), F32),
                   jax.ShapeDtypeStruct((n_seq, 2, W_MIX), F32)],
        compiler_params=pltpu.CompilerParams(
            dimension_semantics=("arbitrary",), vmem_limit_bytes=VMEM_LIMIT),
        name="scans",
    )(*tabs, *([pre] * 10), lru_par, w_lru, s0_hgrn, s0_lru)


def _merge_kernel(n_ctx_tiles, tmod_ref,
                  x_ref, go_ref, sb_ref, sc_ref, sx_ref, rg_ref, of_ref, ob_ref, hlf_ref, hlb_ref,
                  mod_ref, nmix_ref, wmerge_ref, bmerge_ref, wbr_ref, wout_ref, hnorm_ref, scw_ref,
                  nffn_ref, wrt_ref, br_ref,
                  x1_ref, h2_ref, ri_ref, rw_ref, cnt_ref, base_ref):
    del tmod_ref
    i = pl.program_id(0)

    @pl.when(i == 0)
    def _init():
        base_ref[...] = jnp.zeros_like(base_ref)

    mod = mod_ref[0]
    x = x_ref[...]
    h = _rms(x, nmix_ref[...]) * (1.0 + mod[:, D_MODEL:2 * D_MODEL]) + mod[:, 0:D_MODEL]
    gates = _sigmoid(_dot(h.astype(BF16), wmerge_ref[...]) + bmerge_ref[...])

    o = of_ref[...] + ob_ref[...]
    go = go_ref[...]
    parts = []
    for hh in range(N_HEADS):
        sl = slice(D_HEAD * hh, D_HEAD * (hh + 1))
        parts.append(_rms(o[:, sl], hnorm_ref[:, sl]))
    ya = jnp.concatenate(parts, axis=1) * (go * _sigmoid(go))

    lm1 = jnp.where(i < n_ctx_tiles, TM - 1, GRID_W - 1)
    pos = lax.broadcasted_iota(I32, (TM, W_MIX), 0) & lm1
    u = sc_ref[...] * sx_ref[...]
    conv = (scw_ref[1:2, :] * u
            + scw_ref[0:1, :] * jnp.where(pos >= 1, _shift_rows(u, 1), 0.0)
            + scw_ref[2:3, :] * jnp.where(pos < lm1, _shift_rows(u, -1), 0.0))
    yb = sb_ref[...] * conv

    yc = (hlf_ref[...] + hlb_ref[...]) * jax.nn.gelu(rg_ref[...])

    merged = (gates[:, 0:D_MODEL] * _dot(ya.astype(BF16), wbr_ref[0])
              + gates[:, D_MODEL:2 * D_MODEL] * _dot(yb.astype(BF16), wbr_ref[1])
              + gates[:, 2 * D_MODEL:3 * D_MODEL] * _dot(yc.astype(BF16), wbr_ref[2]))
    y = _dot(merged.astype(BF16), wout_ref[...])
    x1 = x + mod[:, 2 * D_MODEL:3 * D_MODEL] * y
    x1_ref[...] = x1

    h2 = _rms(x1, nffn_ref[...]) * (1.0 + mod[:, 4 * D_MODEL:5 * D_MODEL]) + mod[:, 3 * D_MODEL:4 * D_MODEL]
    h2_ref[...] = h2

    h_hi = h2.astype(BF16)
    h_lo = (h2 - h_hi.astype(F32)).astype(BF16)
    wr = wrt_ref[...]
    w_hi = wr.astype(BF16)
    w_lo = (wr - w_hi.astype(F32)).astype(BF16)
    logits = _dot_nt(w_hi, h_hi) + _dot_nt(w_hi, h_lo) + _dot_nt(w_lo, h_hi) + br_ref[...]

    erow = lax.broadcasted_iota(I32, (N_EXPERTS, TM), 0).astype(F32)
    r8 = lax.broadcasted_iota(I32, (8, TM), 0)
    tri = (lax.broadcasted_iota(I32, (TM, TM), 0) <= lax.broadcasted_iota(I32, (TM, TM), 1)).astype(BF16)
    base = base_ref[...]
    top_v, hots, idxs = [], [], []
    work = logits
    for _ in range(TOP_K):
        m = jnp.max(work, axis=0, keepdims=True)
        idx = jnp.min(jnp.where(work == m, erow, float(N_EXPERTS)), axis=0, keepdims=True)
        hot = erow == idx
        work = jnp.where(hot, -jnp.inf, work)
        top_v.append(m)
        hots.append(hot)
        idxs.append(idx)
    sel = hots[0] | hots[1] | hots[2] | hots[3]
    selb = sel.astype(BF16)
    incl = _dot(selb, tri)
    total = _dot(selb, jnp.ones((TM, TM), BF16))
    rank_all = base + incl - 1.0
    ex = [jnp.exp(v - top_v[0]) for v in top_v]
    denom = ex[0] + ex[1] + ex[2] + ex[3]
    ri = jnp.zeros((8, TM), F32)
    rw = jnp.zeros((8, TM), F32)
    for k in range(TOP_K):
        rank_k = jnp.sum(jnp.where(hots[k], rank_all, 0.0), axis=0, keepdims=True)
        ri = jnp.where(r8 == k, idxs[k], ri)
        ri = jnp.where(r8 == TOP_K + k, rank_k, ri)
        rw = jnp.where(r8 == k, ex[k] / denom, rw)
    ri_ref[...] = ri.astype(I32)
    rw_ref[...] = rw
    base = base + total
    base_ref[...] = base
    cnt_ref[...] = base[:, 0:128]


def _merge(x, p, scan_out, tile_mod, mod, lw, n_ctx_tiles):
    n = x.shape[0]
    o_f, hl_f, o_b, hl_b = scan_out

    def tok(width):
        return pl.BlockSpec((TM, width), lambda i, tm: (i, 0))

    def pcol(c):
        return pl.BlockSpec((TM, W_MIX), lambda i, tm: (i, c))

    grid_spec = pltpu.PrefetchScalarGridSpec(
        num_scalar_prefetch=1,
        grid=(n // TM,),
        in_specs=[tok(D_MODEL), pcol(C_GO), pcol(C_SB), pcol(C_SC), pcol(C_SX), pcol(C_RG),
                  tok(W_MIX), tok(W_MIX), tok(W_MIX), tok(W_MIX),
                  pl.BlockSpec((1, 1, N_MOD * D_MODEL), lambda i, tm: (tm[i], 0, 0)),
                  _const_spec((1, D_MODEL)),
                  _const_spec((D_MODEL, 3 * D_MODEL)), _const_spec((1, 3 * D_MODEL)),
                  _const_spec((3, W_MIX, D_MODEL)), _const_spec((D_MODEL, D_MODEL)),
                  _const_spec((1, W_MIX)), _const_spec((3, W_MIX)),
                  _const_spec((1, D_MODEL)), _const_spec((N_EXPERTS, D_MODEL)),
                  _const_spec((N_EXPERTS, 1))],
        out_specs=[tok(D_MODEL), tok(D_MODEL),
                   pl.BlockSpec((8, TM), lambda i, tm: (0, i)),
                   pl.BlockSpec((8, TM), lambda i, tm: (0, i)),
                   pl.BlockSpec((N_EXPERTS, 128), lambda i, tm: (0, 0))],
        scratch_shapes=[pltpu.VMEM((N_EXPERTS, TM), F32)],
    )
    return pl.pallas_call(
        functools.partial(_merge_kernel, n_ctx_tiles),
        grid_spec=grid_spec,
        out_shape=[jax.ShapeDtypeStruct((n, D_MODEL), F32), jax.ShapeDtypeStruct((n, D_MODEL), F32),
                   jax.ShapeDtypeStruct((8, n), I32), jax.ShapeDtypeStruct((8, n), F32),
                   jax.ShapeDtypeStruct((N_EXPERTS, 128), F32)],
        compiler_params=pltpu.CompilerParams(
            dimension_semantics=("arbitrary",), vmem_limit_bytes=VMEM_LIMIT),
        name="merge_router",
    )(tile_mod, x, p, p, p, p, p, o_f, o_b, hl_f, hl_b, mod,
      lw["norm_mix"], lw["w_merge"], lw["b_merge"], lw["w_branch"], lw["w_out"],
      lw["hgrn_norm"], lw["sconv_w"], lw["norm_ffn"], lw["w_router_t"], lw["b_router"])


def _row_copy(src, src_row, dst, dst_row, sem):
    return pltpu.make_async_copy(src.at[pl.ds(src_row, 1)], dst.at[pl.ds(dst_row, 1)], sem)


def _dispatch_kernel(ztile_ref, slot_ref, h2_ref, xs_ref, zeros_ref, sem):
    @pl.when(pl.program_id(0) == 0)
    def _zero_group_tails():
        zeros_ref[...] = jnp.zeros_like(zeros_ref)
        for e in range(2 * N_EXPERTS):
            @pl.when(ztile_ref[e] >= 0)
            def _start(e=e):
                start = pl.multiple_of(ztile_ref[e], TM)
                pltpu.make_async_copy(zeros_ref, xs_ref.at[pl.ds(start, TM)], sem).start()
        for e in range(2 * N_EXPERTS):
            @pl.when(ztile_ref[e] >= 0)
            def _wait():
                pltpu.make_async_copy(zeros_ref, xs_ref.at[pl.ds(0, TM)], sem).wait()

    def issue(r, carry):
        for k in range(TOP_K):
            _row_copy(h2_ref, r, xs_ref, slot_ref[0, 0, k * TM + r], sem).start(priority=k % 2)
        return carry

    lax.fori_loop(0, TM, issue, 0, unroll=DMA_UNROLL)

    def drain(r, carry):
        for k in range(TOP_K):
            _row_copy(h2_ref, 0, xs_ref, 0, sem).wait()
        return carry

    lax.fori_loop(0, TM, drain, 0, unroll=DMA_UNROLL)


def _dispatch(zero_tiles, slots, h2, n_slots):
    n = h2.shape[0]
    grid_spec = pltpu.PrefetchScalarGridSpec(
        num_scalar_prefetch=1,
        grid=(n // TM,),
        in_specs=[pl.BlockSpec((1, 1, TOP_K * TM), lambda i, zt: (i, 0, 0), memory_space=pltpu.SMEM),
                  pl.BlockSpec((TM, D_MODEL), lambda i, zt: (i, 0))],
        out_specs=pl.BlockSpec(memory_space=pl.ANY),
        scratch_shapes=[pltpu.VMEM((TM, D_MODEL), F32), pltpu.SemaphoreType.DMA],
    )
    return pl.pallas_call(
        _dispatch_kernel,
        grid_spec=grid_spec,
        out_shape=jax.ShapeDtypeStruct((n_slots, D_MODEL), F32),
        compiler_params=pltpu.CompilerParams(dimension_semantics=("arbitrary",)),
        name="moe_dispatch",
    )(zero_tiles, slots, h2)


def _ffn_kernel(layer, te_ref, nused_ref, first_ref, last_ref, nxt_ref, par_ref,
                xs_ref, wgu_hbm, bgu_ref, wd_hbm, bd_ref, ys_ref,
                stage_gu, stage_d, wgu_bf, wd_bf, sem):
    i = pl.program_id(0)
    used = i < nused_ref[0]

    def weight_copies(e):
        return (pltpu.make_async_copy(wgu_hbm.at[layer, e], stage_gu, sem.at[0]),
                pltpu.make_async_copy(wd_hbm.at[layer, e], stage_d, sem.at[1]))

    def cast_into(slot):
        wgu_bf[slot] = stage_gu[...].astype(BF16)
        wd_bf[slot] = stage_d[...].astype(BF16)

    @pl.when(used)
    def _compute():
        par = par_ref[i]

        @pl.when(i == 0)
        def _load_first():
            for c in weight_copies(te_ref[0]):
                c.start()
            for c in weight_copies(te_ref[0]):
                c.wait()
            cast_into(par)

        has_next = nxt_ref[i] >= 0

        @pl.when((first_ref[i] == 1) & has_next)
        def _prefetch_next():
            for c in weight_copies(nxt_ref[i]):
                c.start()

        gu = _dot(xs_ref[...].astype(BF16), wgu_bf[par]) + bgu_ref[0, 0]
        glu = jnp.minimum(gu[:, 0:D_FF], SWIGLU_LIMIT)
        lin = jnp.clip(gu[:, D_FF:2 * D_FF], -SWIGLU_LIMIT, SWIGLU_LIMIT)
        act = glu * _sigmoid(SWIGLU_ALPHA * glu) * (lin + 1.0)
        ys_ref[...] = _dot(act.astype(BF16), wd_bf[par]) + bd_ref[0, 0]

        @pl.when((last_ref[i] == 1) & has_next)
        def _stage_next():
            for c in weight_copies(nxt_ref[i]):
                c.wait()
            cast_into(1 - par)

    @pl.when(jnp.logical_not(used))
    def _skip():
        ys_ref[...] = jnp.zeros_like(ys_ref)


def _ffn(layer, tabs, xs, w_gate_up, b_gate_up, w_down, b_down):
    rows = xs.shape[0]
    depth = w_gate_up.shape[0]
    grid_spec = pltpu.PrefetchScalarGridSpec(
        num_scalar_prefetch=6,
        grid=(rows // TM,),
        in_specs=[
            pl.BlockSpec((TM, D_MODEL), lambda i, te, nu, *_: (jnp.minimum(i, nu[0] - 1), 0)),
            pl.BlockSpec(memory_space=pl.ANY),
            pl.BlockSpec((1, 1, 1, 2 * D_FF), lambda i, te, nu, *_: (layer, te[i], 0, 0)),
            pl.BlockSpec(memory_space=pl.ANY),
            pl.BlockSpec((1, 1, 1, D_MODEL), lambda i, te, nu, *_: (layer, te[i], 0, 0)),
        ],
        out_specs=pl.BlockSpec((TM, D_MODEL), lambda i, te, nu, *_: (i, 0)),
        scratch_shapes=[pltpu.VMEM((D_MODEL, 2 * D_FF), F32), pltpu.VMEM((D_FF, D_MODEL), F32),
                        pltpu.VMEM((2, D_MODEL, 2 * D_FF), BF16), pltpu.VMEM((2, D_FF, D_MODEL), BF16),
                        pltpu.SemaphoreType.DMA((2,))],
    )
    return pl.pallas_call(
        functools.partial(_ffn_kernel, layer),
        grid_spec=grid_spec,
        out_shape=jax.ShapeDtypeStruct((rows, D_MODEL), F32),
        compiler_params=pltpu.CompilerParams(
            dimension_semantics=("arbitrary",), vmem_limit_bytes=VMEM_LIMIT),
        name="moe_ffn",
    )(*tabs, xs, w_gate_up, b_gate_up.reshape(depth, N_EXPERTS, 1, 2 * D_FF),
      w_down, b_down.reshape(depth, N_EXPERTS, 1, D_MODEL))


def _combine_kernel(final, tmod_ref, slot_ref, slot_next_ref, x1_ref, rw_ref, mod_ref, nfin_ref, ys_ref,
                    out_ref, buf, sem):
    del tmod_ref
    i = pl.program_id(0)
    cur = i % 2

    def gather(slots, which):
        def issue(r, carry):
            for k in range(TOP_K):
                pltpu.make_async_copy(ys_ref.at[pl.ds(slots[0, 0, k * TM + r], 1)],
                                      buf.at[which, k, pl.ds(r, 1)], sem.at[which]).start(priority=k % 2)
            return carry
        lax.fori_loop(0, TM, issue, 0, unroll=DMA_UNROLL)

    @pl.when(i == 0)
    def _first():
        gather(slot_ref, 0)

    @pl.when(i + 1 < pl.num_programs(0))
    def _next():
        gather(slot_next_ref, 1 - cur)

    def drain(r, carry):
        for k in range(TOP_K):
            pltpu.make_async_copy(ys_ref.at[pl.ds(0, 1)], buf.at[cur, 0, pl.ds(0, 1)], sem.at[cur]).wait()
        return carry

    lax.fori_loop(0, TM, drain, 0, unroll=DMA_UNROLL)

    rw = rw_ref[...]
    acc = rw[:, 0:1] * buf[cur, 0]
    for k in range(1, TOP_K):
        acc = acc + rw[:, k:k + 1] * buf[cur, k]
    out = x1_ref[...] + mod_ref[0][:, 5 * D_MODEL:6 * D_MODEL] * acc
    if final:
        out = _rms(out, nfin_ref[...])
    out_ref[...] = out


def _combine(tile_mod, slots, x1, rw_t, mod, norm_final, ys, final):
    n = x1.shape[0]
    last = n // TM - 1
    grid_spec = pltpu.PrefetchScalarGridSpec(
        num_scalar_prefetch=1,
        grid=(n // TM,),
        in_specs=[pl.BlockSpec((1, 1, TOP_K * TM), lambda i, tm: (i, 0, 0), memory_space=pltpu.SMEM),
                  pl.BlockSpec((1, 1, TOP_K * TM), lambda i, tm: (jnp.minimum(i + 1, last), 0, 0),
                               memory_space=pltpu.SMEM),
                  pl.BlockSpec((TM, D_MODEL), lambda i, tm: (i, 0)),
                  pl.BlockSpec((TM, 8), lambda i, tm: (i, 0)),
                  pl.BlockSpec((1, 1, N_MOD * D_MODEL), lambda i, tm: (tm[i], 0, 0)),
                  pl.BlockSpec((1, D_MODEL), lambda i, tm: (0, 0)),
                  pl.BlockSpec(memory_space=pl.ANY)],
        out_specs=pl.BlockSpec((TM, D_MODEL), lambda i, tm: (i, 0)),
        scratch_shapes=[pltpu.VMEM((2, TOP_K, TM, D_MODEL), F32), pltpu.SemaphoreType.DMA((2,))],
    )
    return pl.pallas_call(
        functools.partial(_combine_kernel, final),
        grid_spec=grid_spec,
        out_shape=jax.ShapeDtypeStruct((n, D_MODEL), F32),
        compiler_params=pltpu.CompilerParams(
            dimension_semantics=("arbitrary",), vmem_limit_bytes=VMEM_LIMIT),
        name="moe_combine",
    )(tile_mod, slots, slots, x1, rw_t, mod, norm_final, ys)


def _moe(layer, h2, ri, rw, counts, x1, tile_mod, mod, norm_final, lw, final):
    n = h2.shape[0]
    n_tiles = n // TM
    n_slots = n * TOP_K + N_EXPERTS * TM
    cnt = counts[:, 0].astype(I32)
    padded = ((cnt + TM - 1) // TM) * TM
    ends = jnp.cumsum(padded)
    offsets = (ends - padded).astype(I32)
    n_used = (ends[-1] // TM).astype(I32)
    tile_ids = jnp.minimum(jnp.arange(n_slots // TM, dtype=I32), n_used - 1)
    tile_expert = jnp.sum((ends[None, :] // TM <= tile_ids[:, None]).astype(I32), axis=1)
    onehot = ri[0:TOP_K, :, None] == jnp.arange(N_EXPERTS, dtype=I32)
    slot = jnp.sum(jnp.where(onehot, offsets, 0), axis=-1) + ri[TOP_K:2 * TOP_K]
    slots = slot.reshape(TOP_K, n_tiles, TM).transpose(1, 0, 2).reshape(n_tiles, 1, TOP_K * TM)
    tail = n_used + jnp.arange(N_EXPERTS, dtype=I32)
    zero_tiles = jnp.concatenate([jnp.where(padded > 0, ends - TM, -1),
                                  jnp.where(tail < n_slots // TM, tail * TM, -1)]).astype(I32)

    prev_e = jnp.concatenate([jnp.full((1,), -1, I32), tile_expert[:-1]])
    first = (tile_expert != prev_e).astype(I32)
    next_tile = (ends // TM)[tile_expert]
    last = (tile_ids + 1 == next_tile).astype(I32)
    nxt = jnp.where(next_tile < n_used, tile_expert[jnp.minimum(next_tile, n_slots // TM - 1)], -1).astype(I32)
    parity = ((jnp.cumsum(first) - 1) & 1).astype(I32)
    ffn_tabs = (tile_expert, n_used.reshape(1), first, last, nxt, parity)

    xs = _dispatch(zero_tiles, slots, h2, n_slots)
    ys = _ffn(layer, ffn_tabs, xs, lw["w_gate_up"], lw["b_gate_up"], lw["w_down"], lw["b_down"])
    return _combine(tile_mod, slots, x1, rw.T, mod, norm_final, ys, final)


def _block_diag_pairs(w):
    z = jnp.zeros_like(w[:, 0::2])
    top = jnp.concatenate([w[:, 0::2], z], axis=-1)
    bot = jnp.concatenate([z, w[:, 1::2]], axis=-1)
    return jnp.concatenate([top, bot], axis=-2)


def kernel(x_prompt, x_sample, state_hgrn, state_rglru, c, c_ctx, norm_mix_w, norm_ffn_w, norm_final_w, w_mod, b_mod, w_in, hgrn_lb_logits, hgrn_norm_w, sconv_w, lru_conv_w, lru_conv_b, lru_wa, lru_ba, lru_wi, lru_bi, lru_lambda, w_branch, w_merge, b_merge, w_out, w_router, b_router, w_gate_up, b_gate_up, w_down, b_down):
    depth = w_in.shape[0]
    bp, tp, _ = x_prompt.shape
    bs, ts, _ = x_sample.shape
    assert tp == TM and ts % TM == 0 and bs + 1 <= MOD_ROWS
    n_ctx = bp * tp
    n_ctx_tiles = n_ctx // TM
    nt_lat = ts // TM

    x = jnp.concatenate([x_prompt.reshape(n_ctx, D_MODEL), x_sample.reshape(bs * ts, D_MODEL)], axis=0)
    n_tiles = x.shape[0] // TM

    lb_cum = jnp.cumsum(jax.nn.softmax(hgrn_lb_logits.astype(F32), axis=0), axis=0)
    lower_bounds = lb_cum - lb_cum[0:1]

    cc = jnp.zeros((MOD_ROWS, D_MODEL), F32).at[0].set(c_ctx).at[1:1 + bs].set(c)
    mod_all = _modulation(cc, w_mod, b_mod)
    tile_mod = jnp.concatenate([jnp.zeros((n_ctx_tiles,), I32),
                                1 + jnp.arange(bs * nt_lat, dtype=I32) // nt_lat])

    lat = jnp.arange(bs * nt_lat, dtype=I32)
    lat_b, lat_c = lat // nt_lat, lat % nt_lat
    ctx = jnp.arange(bp, dtype=I32)
    one = jnp.ones_like(ctx)
    tabs = (
        jnp.concatenate([ctx, n_ctx_tiles + lat]),
        jnp.concatenate([ctx, n_ctx_tiles + lat_b * nt_lat + (nt_lat - 1 - lat_c)]),
        jnp.concatenate([one, (lat_c == 0).astype(I32)]),
        jnp.concatenate([one, (lat_c == nt_lat - 1).astype(I32)]),
        jnp.concatenate([ctx, bp + lat_b]),
    )

    new_hgrn, new_lru = [], []
    for l in range(depth):
        lw = dict(
            norm_mix=norm_mix_w[l].reshape(1, D_MODEL), norm_ffn=norm_ffn_w[l].reshape(1, D_MODEL),
            w_merge=w_merge[l].astype(BF16), b_merge=b_merge[l].reshape(1, 3 * D_MODEL),
            w_branch=w_branch[l].astype(BF16), w_out=w_out[l].astype(BF16),
            hgrn_norm=hgrn_norm_w[l].reshape(1, W_MIX), sconv_w=sconv_w[l],
            w_router_t=w_router[l].T, b_router=b_router[l].reshape(N_EXPERTS, 1),
            w_gate_up=w_gate_up, b_gate_up=b_gate_up, w_down=w_down, b_down=b_down)
        mod = mod_all[l].reshape(MOD_ROWS, 1, N_MOD * D_MODEL)

        lru_par = jnp.concatenate(
            [lru_conv_w[l], lru_conv_b[l][None], lru_ba[l], lru_bi[l], lru_lambda[l],
             jnp.zeros((5, W_MIX), F32)], axis=0)
        p = _premix(x, tile_mod, mod, lw["norm_mix"], w_in[l].astype(BF16), lower_bounds[l], lru_par, n_ctx_tiles)
        w_lru = jnp.concatenate([_block_diag_pairs(lru_wa[l]), _block_diag_pairs(lru_wi[l])], axis=-1).astype(BF16)
        s0_hgrn = jnp.concatenate([jnp.zeros((bp,) + state_hgrn.shape[2:], F32), state_hgrn[:, l]], axis=0)
        s0_lru = jnp.concatenate([jnp.zeros((bp, 2, W_MIX), F32), state_rglru[:, l]], axis=0)
        o_f, hl_f, o_b, hl_b, s_hgrn, s_lru = _scans(p, tabs, lru_par, w_lru, s0_hgrn, s0_lru)
        new_hgrn.append(s_hgrn[:bp])
        new_lru.append(s_lru[:bp])

        x1, h2, ri, rw, counts = _merge(x, p, (o_f, hl_f, o_b, hl_b), tile_mod, mod, lw, n_ctx_tiles)
        x = _moe(l, h2, ri, rw, counts, x1, tile_mod, mod, norm_final_w.reshape(1, D_MODEL), lw,
                 final=(l == depth - 1))

    y_prompt = x[:n_ctx].reshape(bp, tp, D_MODEL)
    y_sample = x[n_ctx:].reshape(bs, ts, D_MODEL)
    return (y_prompt, y_sample, jnp.stack(new_hgrn, axis=1), jnp.stack(new_lru, axis=1))
```

```python
import functools

import jax
import jax.numpy as jnp
from jax import lax
from jax.experimental import pallas as pl
from jax.experimental.pallas import tpu as pltpu

F32 = jnp.float32
BF16 = jnp.bfloat16
I32 = jnp.int32

D_MODEL = 1024
W_MIX = 512
N_HEADS = 4
D_HEAD = 128
N_IN = 10
N_MOD = 6
EPS = 1e-6
LRU_C = 8.0
N_EXPERTS = 32
TOP_K = 4
D_FF = 1024
SWIGLU_LIMIT = 7.0
SWIGLU_ALPHA = 1.702

TM = 256
CHUNK = 64
SUB = 16
N_SUB = CHUNK // SUB
SUB_SHIFT = SUB.bit_length() - 1
HALVES = SUB // 8
DIAG_GROUP = 8
N_CHUNK = TM // CHUNK
ROWS = 8
LOG2E = 1.4426950408889634
DMA_UNROLL = 8
GRID_W = 64
MOD_ROWS = 8
VMEM_LIMIT = 56 * 1024 * 1024


def _dot(a, b):
    return jnp.dot(a, b, preferred_element_type=F32)


def _dot_nt(a, b):
    return lax.dot_general(a, b, (((1,), (1,)), ((), ())), preferred_element_type=F32)


def _sigmoid(x):
    return 0.5 * jnp.tanh(0.5 * x) + 0.5


def _rms(x, w):
    return x * lax.rsqrt(jnp.mean(x * x, axis=-1, keepdims=True) + EPS) * w


def _split3(x):
    a = x.astype(BF16)
    r = x - a.astype(F32)
    b = r.astype(BF16)
    c = (r - b.astype(F32)).astype(BF16)
    return a, b, c


def _const_spec(shape):
    n = len(shape)
    return pl.BlockSpec(shape, lambda *_: (0,) * n, pipeline_mode=pl.Buffered(1))


def _mod_kernel(c_ref, w_ref, b_ref, o_ref):
    c = c_ref[...]
    s = (c * _sigmoid(c)).astype(BF16)
    o_ref[0] = _dot(s, w_ref[0].astype(BF16)) + b_ref[0]


def _modulation(cc, w_mod, b_mod):
    depth, _, width = w_mod.shape
    nb = 1536
    return pl.pallas_call(
        _mod_kernel,
        grid=(depth, width // nb),
        in_specs=[
            pl.BlockSpec((MOD_ROWS, D_MODEL), lambda l, j: (0, 0)),
            pl.BlockSpec((1, D_MODEL, nb), lambda l, j: (l, 0, j)),
            pl.BlockSpec((1, 1, nb), lambda l, j: (l, 0, j)),
        ],
        out_specs=pl.BlockSpec((1, MOD_ROWS, nb), lambda l, j: (l, 0, j)),
        out_shape=jax.ShapeDtypeStruct((depth, MOD_ROWS, width), F32),
        compiler_params=pltpu.CompilerParams(
            dimension_semantics=("arbitrary", "arbitrary"), vmem_limit_bytes=VMEM_LIMIT),
        name="modulation",
    )(cc, w_mod, b_mod.reshape(depth, 1, width))


C_V, C_GO, C_SB, C_SC, C_SX, C_RG, C_QS, C_KF, C_GF, C_KB, C_GB, C_XR = range(12)


def _premix_kernel(n_ctx_tiles, tmod_ref, x_ref, mod_ref, nw_ref, win_ref, lb_ref, lp_ref, o_ref):
    del tmod_ref
    i = pl.program_id(0)
    mod = mod_ref[0]
    h = _rms(x_ref[...], nw_ref[...]) * (1.0 + mod[:, D_MODEL:2 * D_MODEL]) + mod[:, 0:D_MODEL]
    p = _dot(h.astype(BF16), win_ref[...])

    def blk(c):
        return p[:, W_MIX * c:W_MIX * (c + 1)]

    def put(c, val):
        o_ref[:, W_MIX * c:W_MIX * (c + 1)] = val

    for c_out, c_in in ((C_V, 3), (C_GO, 4), (C_SB, 5), (C_SC, 6), (C_SX, 7), (C_RG, 9)):
        put(c_out, blk(c_in))

    q = blk(0)
    put(C_QS, q * _sigmoid(q) * (D_HEAD ** -0.5))
    for d, (c_k, c_g) in enumerate(((C_KF, C_GF), (C_KB, C_GB))):
        f = blk(1 + d)
        e = jnp.exp(-jnp.abs(f))
        r = 1.0 / (1.0 + e)
        sig_neg = jnp.where(f >= 0, e * r, r)
        log_sig = jnp.minimum(f, 0.0) + jnp.log(r)
        lb = lb_ref[d:d + 1, :]
        put(c_k, (1.0 - lb) * sig_neg)
        la = jnp.log(lb)
        lbb = jnp.log1p(-lb) + log_sig
        put(c_g, jnp.maximum(la, lbb) + jnp.log(1.0 + jnp.exp(-jnp.abs(la - lbb))))

    lm1 = jnp.where(i < n_ctx_tiles, TM - 1, GRID_W - 1)
    pos = lax.broadcasted_iota(I32, (TM, W_MIX), 0) & lm1
    u = blk(8)
    put(C_XR, lp_ref[2:3, :] * u + lp_ref[4:5, :]
        + lp_ref[0:1, :] * jnp.where(pos >= 2, _shift_rows(u, 2), 0.0)
        + lp_ref[1:2, :] * jnp.where(pos >= 1, _shift_rows(u, 1), 0.0)
        + lp_ref[3:4, :] * jnp.where(pos < lm1, _shift_rows(u, -1), 0.0))


def _premix(x, tile_mod, mod, norm_w, w_in_bf, lb, lru_par, n_ctx_tiles):
    n = x.shape[0]
    width = 12 * W_MIX
    grid_spec = pltpu.PrefetchScalarGridSpec(
        num_scalar_prefetch=1,
        grid=(n // TM,),
        in_specs=[
            pl.BlockSpec((TM, D_MODEL), lambda i, tm: (i, 0)),
            pl.BlockSpec((1, 1, N_MOD * D_MODEL), lambda i, tm: (tm[i], 0, 0)),
            _const_spec((1, D_MODEL)),
            _const_spec(w_in_bf.shape),
            _const_spec((2, W_MIX)),
            _const_spec((16, W_MIX)),
        ],
        out_specs=pl.BlockSpec((TM, width), lambda i, tm: (i, 0)),
    )
    return pl.pallas_call(
        functools.partial(_premix_kernel, n_ctx_tiles),
        grid_spec=grid_spec,
        out_shape=jax.ShapeDtypeStruct((n, width), F32),
        compiler_params=pltpu.CompilerParams(
            dimension_semantics=("arbitrary",), vmem_limit_bytes=VMEM_LIMIT),
        name="premix",
    )(tile_mod, x, mod, norm_w, w_in_bf, lb, lru_par)


def _shift_rows(x, s):
    return pltpu.roll(x, s % x.shape[0], axis=0)


def _lru_scan(a, b, h0, reverse):
    n, width = a.shape
    nblk = n // ROWS
    a = a.reshape(nblk, ROWS, width)
    b = b.reshape(nblk, ROWS, width)
    row = lax.broadcasted_iota(I32, a.shape, 1)
    s = 1
    while s < ROWS:
        keep = (row < ROWS - s) if reverse else (row >= s)
        shift = (ROWS - s) if reverse else s
        a_sh = jnp.where(keep, pltpu.roll(a, shift, axis=1), 1.0)
        b_sh = jnp.where(keep, pltpu.roll(b, shift, axis=1), 0.0)
        b = a * b_sh + b
        a = a * a_sh
        s *= 2
    edge = 0 if reverse else ROWS - 1
    out = [None] * nblk
    carry = jnp.broadcast_to(h0, (ROWS, width))
    for j in (reversed(range(nblk)) if reverse else range(nblk)):
        hj = a[j] * carry + b[j]
        out[j] = hj
        carry = jnp.broadcast_to(hj[edge:edge + 1, :], (ROWS, width))
    return jnp.concatenate(out, axis=0)


def _pair_blocks(x):
    zero = jnp.zeros((x.shape[0], D_HEAD), x.dtype)
    return jnp.concatenate([jnp.concatenate([x[:, :D_HEAD], zero], axis=1),
                            jnp.concatenate([zero, x[:, D_HEAD:]], axis=1)], axis=0)


def _scan_kernel(tf_ref, tb_ref, first_ref, last_ref, sb_ref,
                 qf_ref, kf_ref, gf_ref, vf_ref, xrf_ref, qb_ref, kb_ref, gb_ref, vb_ref, xrb_ref,
                 lp_ref, wlru_ref, s0h_ref, s0l_ref,
                 of_ref, hlf_ref, ob_ref, hlb_ref, sfh_ref, sfl_ref,
                 st_ref, hcar_ref, lcs_ref):
    del tf_ref, tb_ref, sb_ref
    step = pl.program_id(0)

    @pl.when(first_ref[step] == 1)
    def _init():
        for d in range(2):
            for h in range(N_HEADS):
                st_ref[d, h] = s0h_ref[0, d, h].T
        hcar_ref[0:2, :] = s0l_ref[0]

    for d, (xr_ref, hl_ref) in enumerate(((xrf_ref, hlf_ref), (xrb_ref, hlb_ref))):
        xr = xr_ref[...]
        rl, il = [], []
        for nblk in range(W_MIX // 128):
            z = _dot(xr[:, 128 * nblk:128 * (nblk + 1)].astype(BF16), wlru_ref[d, nblk])
            rl.append(z[:, :128])
            il.append(z[:, 128:])
        r_gate = _sigmoid(jnp.concatenate(rl, axis=1) + lp_ref[5 + d:6 + d, :])
        i_gate = _sigmoid(jnp.concatenate(il, axis=1) + lp_ref[7 + d:8 + d, :])
        lam = lp_ref[9 + d:10 + d, :]
        softplus = jnp.maximum(-lam, 0.0) + jnp.log1p(jnp.exp(-jnp.abs(lam)))
        log_a = (-LRU_C) * r_gate * softplus
        a = jnp.exp(log_a)
        th = jnp.tanh(log_a)
        bx = jnp.sqrt(-2.0 * th / (1.0 - th)) * i_gate * xr
        hd = _lru_scan(a, bx, hcar_ref[d:d + 1, :], reverse=(d == 1))
        hl_ref[...] = hd
        edge = TM - 1 if d == 0 else 0
        hcar_ref[d:d + 1, :] = hd[edge:edge + 1, :]

    r64 = lax.broadcasted_iota(I32, (CHUNK, CHUNK), 0)
    c64 = lax.broadcasted_iota(I32, (CHUNK, CHUNK), 1)
    same_sub = (r64 >> SUB_SHIFT) == (c64 >> SUB_SHIFT)
    hrow = lax.broadcasted_iota(I32, (CHUNK // HALVES, 2 * CHUNK), 0)
    hcol = lax.broadcasted_iota(I32, (CHUNK // HALVES, 2 * CHUNK), 1) & (CHUNK - 1)
    hrel = hcol - ((hrow >> 3) << SUB_SHIFT)
    hmod = hrow & (ROWS - 1)
    q_refs, k_refs, g_refs = (qf_ref, qb_ref), (kf_ref, kb_ref), (gf_ref, gb_ref)
    v_refs = (vf_ref, vb_ref)
    o_refs = (of_ref, ob_ref)

    def chunk_body(ci, carry):
        for d in range(2):
            rev = d == 1
            c0 = pl.multiple_of((N_CHUNK - 1 - ci) * CHUNK if rev else ci * CHUNK, CHUNK)
            order = tuple(reversed(range(N_SUB))) if rev else tuple(range(N_SUB))
            q = q_refs[d][pl.ds(c0, CHUNK), :]
            k = k_refs[d][pl.ds(c0, CHUNK), :]
            g = g_refs[d][pl.ds(c0, CHUNK), :]
            v = v_refs[d][pl.ds(c0, CHUNK), :]

            tri = (same_sub & ((c64 >= r64) if rev else (c64 <= r64))).astype(BF16)
            res = _dot(tri, jnp.concatenate(_split3(g), axis=1))
            lc = res[:, 0:W_MIX] + res[:, W_MIX:2 * W_MIX] + res[:, 2 * W_MIX:3 * W_MIX]
            lc2 = lc * LOG2E
            lcs_ref[d] = lc2
            last = 0 if rev else SUB - 1
            tot_sub = [lc[SUB * a + last:SUB * a + last + 1, :] for a in range(N_SUB)]
            totb = jnp.concatenate([jnp.broadcast_to(t, (SUB, W_MIX)) for t in tot_sub], axis=0)

            qh = q * jnp.exp2(lc2)
            kt = k * jnp.exp(totb - lc)

            tot = [tot_sub[a] for a in order]
            cum = [tot[0]]
            for i in range(1, N_SUB):
                cum.append(cum[-1] + tot[i])

            def sub_rows(x, a):
                return x[SUB * a:SUB * (a + 1), :]

            def by_time(fn):
                rows = [None] * N_SUB
                for i, a in enumerate(order):
                    rows[a] = fn(i, a)
                return jnp.concatenate(rows, axis=0)

            zeros_sub = jnp.zeros((SUB, W_MIX), F32)
            qe = by_time(lambda i, a: sub_rows(qh, a) if i == 0 else sub_rows(qh, a) * jnp.exp(cum[i - 1]))
            kl = by_time(lambda j, a: sub_rows(kt, a) if j == N_SUB - 1
                         else sub_rows(kt, a) * jnp.exp(cum[-1] - cum[j]))
            chunk_decay = jnp.exp(cum[-1])
            lhs, kts = [], []
            for j in range(N_SUB - 1):
                def lhs_rows(i, a, j=j):
                    if i <= j:
                        return zeros_sub
                    if i == j + 1:
                        return sub_rows(qh, a)
                    return sub_rows(qh, a) * jnp.exp(cum[i - 1] - cum[j])
                lhs.append(by_time(lhs_rows).astype(BF16))
                kts.append(by_time(lambda i, a, j=j: sub_rows(kt, a) if i == j else zeros_sub).astype(BF16))

            kb = k.astype(BF16)
            vb = v.astype(BF16)
            zero_state = jnp.zeros((D_HEAD, D_HEAD), F32)
            outs = []
            for hp in range(N_HEADS // 2):
                sl = slice(2 * D_HEAD * hp, 2 * D_HEAD * (hp + 1))
                k_pair = _pair_blocks(kb[:, sl])
                att_half = [jnp.zeros((CHUNK // HALVES, 2 * CHUNK), F32) for _ in range(HALVES)]
                for s0 in range(0, SUB, DIAG_GROUP):
                    pieces, spans = [], []
                    for s in range(s0, s0 + DIAG_GROUP):
                        use = ((True,) if HALVES == 1 else (True, s >= ROWS) if rev else (s < ROWS, True))
                        for half in range(HALVES):
                            if not use[half]:
                                continue
                            spans.append((s, half, len(pieces) * ROWS))
                            for a in range(N_SUB):
                                r0 = SUB * a + ROWS * half
                                lsel = jnp.broadcast_to(lcs_ref[d, SUB * a + s:SUB * a + s + 1, sl],
                                                        (ROWS, 2 * D_HEAD))
                                pieces.append(q[r0:r0 + ROWS, sl] * jnp.exp2(lc2[r0:r0 + ROWS, sl] - lsel))
                    col = _dot_nt(jnp.concatenate(pieces, axis=0).astype(BF16), k_pair)
                    for s, half, off in spans:
                        rm = hmod + ROWS * half
                        dmask = (hrel == s) & ((rm <= s) if rev else (rm >= s))
                        att_half[half] = jnp.where(dmask, col[off:off + CHUNK // HALVES], att_half[half])
                att = jnp.concatenate(
                    [att_half[half][ROWS * a:ROWS * (a + 1)] for a in range(N_SUB) for half in range(HALVES)], axis=0)
                for j in range(N_SUB - 1):
                    att = att + _dot_nt(lhs[j][:, sl], _pair_blocks(kts[j][:, sl]))
                st = [st_ref[d, 2 * hp], st_ref[d, 2 * hp + 1]]
                st_pair = jnp.concatenate([jnp.concatenate([st[0], zero_state], axis=1),
                                           jnp.concatenate([zero_state, st[1]], axis=1)], axis=0)
                outs.append(_dot_nt(qe[:, sl].astype(BF16), st_pair.astype(BF16))
                            + _dot(att.astype(BF16), _pair_blocks(vb[:, sl])))
                for i in range(2):
                    hl = slice(D_HEAD * (2 * hp + i), D_HEAD * (2 * hp + i + 1))
                    st_ref[d, 2 * hp + i] = (st[i] * chunk_decay[:, hl]
                                             + _dot(v[:, hl].T.astype(BF16), kl[:, hl].astype(BF16)))
            o_refs[d][pl.ds(c0, CHUNK), :] = jnp.concatenate(outs, axis=1)
        return carry

    lax.fori_loop(0, N_CHUNK, chunk_body, 0)

    @pl.when(last_ref[step] == 1)
    def _final():
        for d in range(2):
            for h in range(N_HEADS):
                sfh_ref[0, d, h] = st_ref[d, h].T
        sfl_ref[0] = hcar_ref[0:2, :]


def _scans(pre, tabs, lru_par, w_lru, s0_hgrn, s0_lru):
    n = pre.shape[0]
    n_steps = tabs[0].shape[0]
    n_seq = s0_hgrn.shape[0]

    def col(c, which):
        if which == 0:
            return pl.BlockSpec((TM, W_MIX), lambda s, tf, tb, fi, la, sb: (tf[s], c))
        return pl.BlockSpec((TM, W_MIX), lambda s, tf, tb, fi, la, sb: (tb[s], c))

    st_spec = pl.BlockSpec((1, 2, N_HEADS, D_HEAD, D_HEAD), lambda s, tf, tb, fi, la, sb: (sb[s], 0, 0, 0, 0))
    sl_spec = pl.BlockSpec((1, 2, W_MIX), lambda s, tf, tb, fi, la, sb: (sb[s], 0, 0))
    grid_spec = pltpu.PrefetchScalarGridSpec(
        num_scalar_prefetch=5,
        grid=(n_steps,),
        in_specs=[col(C_QS, 0), col(C_KF, 0), col(C_GF, 0), col(C_V, 0), col(C_XR, 0),
                  col(C_QS, 1), col(C_KB, 1), col(C_GB, 1), col(C_V, 1), col(C_XR, 1),
                  _const_spec((16, W_MIX)),
                  _const_spec((2, W_MIX // 128, 128, 256)),
                  st_spec, sl_spec],
        out_specs=[col(0, 0), col(0, 0), col(0, 1), col(0, 1), st_spec, sl_spec],
        scratch_shapes=[
            pltpu.VMEM((2, N_HEADS, D_HEAD, D_HEAD), F32),
            pltpu.VMEM((8, W_MIX), F32),
            pltpu.VMEM((2, CHUNK, W_MIX), F32),
        ],
    )
    row = jax.ShapeDtypeStruct((n, W_MIX), F32)
    return pl.pallas_call(
        _scan_kernel,
        grid_spec=grid_spec,
        out_shape=[row, row, row, row,
                   jax.ShapeDtypeStruct((n_seq, 2, N_HEADS, D_HEAD, D_HEAD), F32),
                   jax.ShapeDtypeStruct((n_seq, 2, W_MIX), F32)],
        compiler_params=pltpu.CompilerParams(
            dimension_semantics=("arbitrary",), vmem_limit_bytes=VMEM_LIMIT),
        name="scans",
    )(*tabs, *([pre] * 10), lru_par, w_lru, s0_hgrn, s0_lru)


def _merge_kernel(n_ctx_tiles, tmod_ref,
                  x_ref, go_ref, sb_ref, sc_ref, sx_ref, rg_ref, of_ref, ob_ref, hlf_ref, hlb_ref,
                  mod_ref, nmix_ref, wmerge_ref, bmerge_ref, wbr_ref, wout_ref, hnorm_ref, scw_ref,
                  nffn_ref, wrt_ref, br_ref,
                  x1_ref, h2_ref, ri_ref, rw_ref, cnt_ref, base_ref):
    del tmod_ref
    i = pl.program_id(0)

    @pl.when(i == 0)
    def _init():
        base_ref[...] = jnp.zeros_like(base_ref)

    mod = mod_ref[0]
    x = x_ref[...]
    h = _rms(x, nmix_ref[...]) * (1.0 + mod[:, D_MODEL:2 * D_MODEL]) + mod[:, 0:D_MODEL]
    gates = _sigmoid(_dot(h.astype(BF16), wmerge_ref[...]) + bmerge_ref[...])

    o = of_ref[...] + ob_ref[...]
    go = go_ref[...]
    parts = []
    for hh in range(N_HEADS):
        sl = slice(D_HEAD * hh, D_HEAD * (hh + 1))
        parts.append(_rms(o[:, sl], hnorm_ref[:, sl]))
    ya = jnp.concatenate(parts, axis=1) * (go * _sigmoid(go))

    lm1 = jnp.where(i < n_ctx_tiles, TM - 1, GRID_W - 1)
    pos = lax.broadcasted_iota(I32, (TM, W_MIX), 0) & lm1
    u = sc_ref[...] * sx_ref[...]
    conv = (scw_ref[1:2, :] * u
            + scw_ref[0:1, :] * jnp.where(pos >= 1, _shift_rows(u, 1), 0.0)
            + scw_ref[2:3, :] * jnp.where(pos < lm1, _shift_rows(u, -1), 0.0))
    yb = sb_ref[...] * conv

    yc = (hlf_ref[...] + hlb_ref[...]) * jax.nn.gelu(rg_ref[...])

    merged = (gates[:, 0:D_MODEL] * _dot(ya.astype(BF16), wbr_ref[0])
              + gates[:, D_MODEL:2 * D_MODEL] * _dot(yb.astype(BF16), wbr_ref[1])
              + gates[:, 2 * D_MODEL:3 * D_MODEL] * _dot(yc.astype(BF16), wbr_ref[2]))
    y = _dot(merged.astype(BF16), wout_ref[...])
    x1 = x + mod[:, 2 * D_MODEL:3 * D_MODEL] * y
    x1_ref[...] = x1

    h2 = _rms(x1, nffn_ref[...]) * (1.0 + mod[:, 4 * D_MODEL:5 * D_MODEL]) + mod[:, 3 * D_MODEL:4 * D_MODEL]
    h2_ref[...] = h2

    h_hi = h2.astype(BF16)
    h_lo = (h2 - h_hi.astype(F32)).astype(BF16)
    wr = wrt_ref[...]
    w_hi = wr.astype(BF16)
    w_lo = (wr - w_hi.astype(F32)).astype(BF16)
    logits = _dot_nt(w_hi, h_hi) + _dot_nt(w_hi, h_lo) + _dot_nt(w_lo, h_hi) + br_ref[...]

    erow = lax.broadcasted_iota(I32, (N_EXPERTS, TM), 0).astype(F32)
    r8 = lax.broadcasted_iota(I32, (8, TM), 0)
    tri = (lax.broadcasted_iota(I32, (TM, TM), 0) <= lax.broadcasted_iota(I32, (TM, TM), 1)).astype(BF16)
    base = base_ref[...]
    top_v, hots, idxs = [], [], []
    work = logits
    for _ in range(TOP_K):
        m = jnp.max(work, axis=0, keepdims=True)
        idx = jnp.min(jnp.where(work == m, erow, float(N_EXPERTS)), axis=0, keepdims=True)
        hot = erow == idx
        work = jnp.where(hot, -jnp.inf, work)
        top_v.append(m)
        hots.append(hot)
        idxs.append(idx)
    sel = hots[0] | hots[1] | hots[2] | hots[3]
    selb = sel.astype(BF16)
    incl = _dot(selb, tri)
    total = _dot(selb, jnp.ones((TM, TM), BF16))
    rank_all = base + incl - 1.0
    ex = [jnp.exp(v - top_v[0]) for v in top_v]
    denom = ex[0] + ex[1] + ex[2] + ex[3]
    ri = jnp.zeros((8, TM), F32)
    rw = jnp.zeros((8, TM), F32)
    for k in range(TOP_K):
        rank_k = jnp.sum(jnp.where(hots[k], rank_all, 0.0), axis=0, keepdims=True)
        ri = jnp.where(r8 == k, idxs[k], ri)
        ri = jnp.where(r8 == TOP_K + k, rank_k, ri)
        rw = jnp.where(r8 == k, ex[k] / denom, rw)
    ri_ref[...] = ri.astype(I32)
    rw_ref[...] = rw
    base = base + total
    base_ref[...] = base
    cnt_ref[...] = base[:, 0:128]


def _merge(x, p, scan_out, tile_mod, mod, lw, n_ctx_tiles):
    n = x.shape[0]
    o_f, hl_f, o_b, hl_b = scan_out

    def tok(width):
        return pl.BlockSpec((TM, width), lambda i, tm: (i, 0))

    def pcol(c):
        return pl.BlockSpec((TM, W_MIX), lambda i, tm: (i, c))

    grid_spec = pltpu.PrefetchScalarGridSpec(
        num_scalar_prefetch=1,
        grid=(n // TM,),
        in_specs=[tok(D_MODEL), pcol(C_GO), pcol(C_SB), pcol(C_SC), pcol(C_SX), pcol(C_RG),
                  tok(W_MIX), tok(W_MIX), tok(W_MIX), tok(W_MIX),
                  pl.BlockSpec((1, 1, N_MOD * D_MODEL), lambda i, tm: (tm[i], 0, 0)),
                  _const_spec((1, D_MODEL)),
                  _const_spec((D_MODEL, 3 * D_MODEL)), _const_spec((1, 3 * D_MODEL)),
                  _const_spec((3, W_MIX, D_MODEL)), _const_spec((D_MODEL, D_MODEL)),
                  _const_spec((1, W_MIX)), _const_spec((3, W_MIX)),
                  _const_spec((1, D_MODEL)), _const_spec((N_EXPERTS, D_MODEL)),
                  _const_spec((N_EXPERTS, 1))],
        out_specs=[tok(D_MODEL), tok(D_MODEL),
                   pl.BlockSpec((8, TM), lambda i, tm: (0, i)),
                   pl.BlockSpec((8, TM), lambda i, tm: (0, i)),
                   pl.BlockSpec((N_EXPERTS, 128), lambda i, tm: (0, 0))],
        scratch_shapes=[pltpu.VMEM((N_EXPERTS, TM), F32)],
    )
    return pl.pallas_call(
        functools.partial(_merge_kernel, n_ctx_tiles),
        grid_spec=grid_spec,
        out_shape=[jax.ShapeDtypeStruct((n, D_MODEL), F32), jax.ShapeDtypeStruct((n, D_MODEL), F32),
                   jax.ShapeDtypeStruct((8, n), I32), jax.ShapeDtypeStruct((8, n), F32),
                   jax.ShapeDtypeStruct((N_EXPERTS, 128), F32)],
        compiler_params=pltpu.CompilerParams(
            dimension_semantics=("arbitrary",), vmem_limit_bytes=VMEM_LIMIT),
        name="merge_router",
    )(tile_mod, x, p, p, p, p, p, o_f, o_b, hl_f, hl_b, mod,
      lw["norm_mix"], lw["w_merge"], lw["b_merge"], lw["w_branch"], lw["w_out"],
      lw["hgrn_norm"], lw["sconv_w"], lw["norm_ffn"], lw["w_router_t"], lw["b_router"])


def _row_copy(src, src_row, dst, dst_row, sem):
    return pltpu.make_async_copy(src.at[pl.ds(src_row, 1)], dst.at[pl.ds(dst_row, 1)], sem)


def _dispatch_kernel(ztile_ref, slot_ref, h2_ref, xs_ref, zeros_ref, sem):
    @pl.when(pl.program_id(0) == 0)
    def _zero_group_tails():
        zeros_ref[...] = jnp.zeros_like(zeros_ref)
        for e in range(2 * N_EXPERTS):
            @pl.when(ztile_ref[e] >= 0)
            def _start(e=e):
                start = pl.multiple_of(ztile_ref[e], TM)
                pltpu.make_async_copy(zeros_ref, xs_ref.at[pl.ds(start, TM)], sem).start()
        for e in range(2 * N_EXPERTS):
            @pl.when(ztile_ref[e] >= 0)
            def _wait():
                pltpu.make_async_copy(zeros_ref, xs_ref.at[pl.ds(0, TM)], sem).wait()

    def issue(r, carry):
        for k in range(TOP_K):
            _row_copy(h2_ref, r, xs_ref, slot_ref[0, 0, k * TM + r], sem).start(priority=k % 2)
        return carry

    lax.fori_loop(0, TM, issue, 0, unroll=DMA_UNROLL)

    def drain(r, carry):
        for k in range(TOP_K):
            _row_copy(h2_ref, 0, xs_ref, 0, sem).wait()
        return carry

    lax.fori_loop(0, TM, drain, 0, unroll=DMA_UNROLL)


def _dispatch(zero_tiles, slots, h2, n_slots):
    n = h2.shape[0]
    grid_spec = pltpu.PrefetchScalarGridSpec(
        num_scalar_prefetch=1,
        grid=(n // TM,),
        in_specs=[pl.BlockSpec((1, 1, TOP_K * TM), lambda i, zt: (i, 0, 0), memory_space=pltpu.SMEM),
                  pl.BlockSpec((TM, D_MODEL), lambda i, zt: (i, 0))],
        out_specs=pl.BlockSpec(memory_space=pl.ANY),
        scratch_shapes=[pltpu.VMEM((TM, D_MODEL), F32), pltpu.SemaphoreType.DMA],
    )
    return pl.pallas_call(
        _dispatch_kernel,
        grid_spec=grid_spec,
        out_shape=jax.ShapeDtypeStruct((n_slots, D_MODEL), F32),
        compiler_params=pltpu.CompilerParams(dimension_semantics=("arbitrary",)),
        name="moe_dispatch",
    )(zero_tiles, slots, h2)


def _ffn_kernel(layer, te_ref, nused_ref, first_ref, last_ref, nxt_ref, par_ref,
                xs_ref, wgu_hbm, bgu_ref, wd_hbm, bd_ref, ys_ref,
                stage_gu, stage_d, wgu_bf, wd_bf, sem):
    i = pl.program_id(0)
    used = i < nused_ref[0]

    def weight_copies(e):
        return (pltpu.make_async_copy(wgu_hbm.at[layer, e], stage_gu, sem.at[0]),
                pltpu.make_async_copy(wd_hbm.at[layer, e], stage_d, sem.at[1]))

    def cast_into(slot):
        wgu_bf[slot] = stage_gu[...].astype(BF16)
        wd_bf[slot] = stage_d[...].astype(BF16)

    @pl.when(used)
    def _compute():
        par = par_ref[i]

        @pl.when(i == 0)
        def _load_first():
            for c in weight_copies(te_ref[0]):
                c.start()
            for c in weight_copies(te_ref[0]):
                c.wait()
            cast_into(par)

        has_next = nxt_ref[i] >= 0

        @pl.when((first_ref[i] == 1) & has_next)
        def _prefetch_next():
            for c in weight_copies(nxt_ref[i]):
                c.start()

        gu = _dot(xs_ref[...].astype(BF16), wgu_bf[par]) + bgu_ref[0, 0]
        glu = jnp.minimum(gu[:, 0:D_FF], SWIGLU_LIMIT)
        lin = jnp.clip(gu[:, D_FF:2 * D_FF], -SWIGLU_LIMIT, SWIGLU_LIMIT)
        act = glu * _sigmoid(SWIGLU_ALPHA * glu) * (lin + 1.0)
        ys_ref[...] = _dot(act.astype(BF16), wd_bf[par]) + bd_ref[0, 0]

        @pl.when((last_ref[i] == 1) & has_next)
        def _stage_next():
            for c in weight_copies(nxt_ref[i]):
                c.wait()
            cast_into(1 - par)

    @pl.when(jnp.logical_not(used))
    def _skip():
        ys_ref[...] = jnp.zeros_like(ys_ref)


def _ffn(layer, tabs, xs, w_gate_up, b_gate_up, w_down, b_down):
    rows = xs.shape[0]
    depth = w_gate_up.shape[0]
    grid_spec = pltpu.PrefetchScalarGridSpec(
        num_scalar_prefetch=6,
        grid=(rows // TM,),
        in_specs=[
            pl.BlockSpec((TM, D_MODEL), lambda i, te, nu, *_: (jnp.minimum(i, nu[0] - 1), 0)),
            pl.BlockSpec(memory_space=pl.ANY),
            pl.BlockSpec((1, 1, 1, 2 * D_FF), lambda i, te, nu, *_: (layer, te[i], 0, 0)),
            pl.BlockSpec(memory_space=pl.ANY),
            pl.BlockSpec((1, 1, 1, D_MODEL), lambda i, te, nu, *_: (layer, te[i], 0, 0)),
        ],
        out_specs=pl.BlockSpec((TM, D_MODEL), lambda i, te, nu, *_: (i, 0)),
        scratch_shapes=[pltpu.VMEM((D_MODEL, 2 * D_FF), F32), pltpu.VMEM((D_FF, D_MODEL), F32),
                        pltpu.VMEM((2, D_MODEL, 2 * D_FF), BF16), pltpu.VMEM((2, D_FF, D_MODEL), BF16),
                        pltpu.SemaphoreType.DMA((2,))],
    )
    return pl.pallas_call(
        functools.partial(_ffn_kernel, layer),
        grid_spec=grid_spec,
        out_shape=jax.ShapeDtypeStruct((rows, D_MODEL), F32),
        compiler_params=pltpu.CompilerParams(
            dimension_semantics=("arbitrary",), vmem_limit_bytes=VMEM_LIMIT),
        name="moe_ffn",
    )(*tabs, xs, w_gate_up, b_gate_up.reshape(depth, N_EXPERTS, 1, 2 * D_FF),
      w_down, b_down.reshape(depth, N_EXPERTS, 1, D_MODEL))


def _combine_kernel(ctx_tiles, tmod_ref, slot_ref, slot_next_ref, x1_ref, rw_ref, mod_ref, nfin_ref, ys_ref,
                    *refs):
    del tmod_ref
    *out_refs, buf, sem = refs
    i = pl.program_id(0)
    cur = i % 2

    def gather(slots, which):
        def issue(r, carry):
            for k in range(TOP_K):
                pltpu.make_async_copy(ys_ref.at[pl.ds(slots[0, 0, k * TM + r], 1)],
                                      buf.at[which, k, pl.ds(r, 1)], sem.at[which]).start(priority=k % 2)
            return carry
        lax.fori_loop(0, TM, issue, 0, unroll=DMA_UNROLL)

    @pl.when(i == 0)
    def _first():
        gather(slot_ref, 0)

    @pl.when(i + 1 < pl.num_programs(0))
    def _next():
        gather(slot_next_ref, 1 - cur)

    def drain(r, carry):
        for k in range(TOP_K):
            pltpu.make_async_copy(ys_ref.at[pl.ds(0, 1)], buf.at[cur, 0, pl.ds(0, 1)], sem.at[cur]).wait()
        return carry

    lax.fori_loop(0, TM, drain, 0, unroll=DMA_UNROLL)

    rw = rw_ref[...]
    acc = rw[:, 0:1] * buf[cur, 0]
    for k in range(1, TOP_K):
        acc = acc + rw[:, k:k + 1] * buf[cur, k]
    out = x1_ref[...] + mod_ref[0][:, 5 * D_MODEL:6 * D_MODEL] * acc
    if not ctx_tiles:
        out_refs[0][...] = out
        return
    out = _rms(out, nfin_ref[...])

    @pl.when(i < ctx_tiles)
    def _context():
        out_refs[0][...] = out

    @pl.when(i >= ctx_tiles)
    def _latent():
        out_refs[1][...] = out


def _combine(tile_mod, slots, x1, rw_t, mod, norm_final, ys, ctx_tiles):
    n = x1.shape[0]
    last = n // TM - 1
    if ctx_tiles:
        out_specs = [pl.BlockSpec((TM, D_MODEL), lambda i, tm: (jnp.minimum(i, ctx_tiles - 1), 0)),
                     pl.BlockSpec((TM, D_MODEL), lambda i, tm: (jnp.maximum(i - ctx_tiles, 0), 0))]
        out_shape = [jax.ShapeDtypeStruct((ctx_tiles * TM, D_MODEL), F32),
                     jax.ShapeDtypeStruct((n - ctx_tiles * TM, D_MODEL), F32)]
    else:
        out_specs = pl.BlockSpec((TM, D_MODEL), lambda i, tm: (i, 0))
        out_shape = jax.ShapeDtypeStruct((n, D_MODEL), F32)
    grid_spec = pltpu.PrefetchScalarGridSpec(
        num_scalar_prefetch=1,
        grid=(n // TM,),
        in_specs=[pl.BlockSpec((1, 1, TOP_K * TM), lambda i, tm: (i, 0, 0), memory_space=pltpu.SMEM),
                  pl.BlockSpec((1, 1, TOP_K * TM), lambda i, tm: (jnp.minimum(i + 1, last), 0, 0),
                               memory_space=pltpu.SMEM),
                  pl.BlockSpec((TM, D_MODEL), lambda i, tm: (i, 0)),
                  pl.BlockSpec((TM, 8), lambda i, tm: (i, 0)),
                  pl.BlockSpec((1, 1, N_MOD * D_MODEL), lambda i, tm: (tm[i], 0, 0)),
                  pl.BlockSpec((1, D_MODEL), lambda i, tm: (0, 0)),
                  pl.BlockSpec(memory_space=pl.ANY)],
        out_specs=out_specs,
        scratch_shapes=[pltpu.VMEM((2, TOP_K, TM, D_MODEL), F32), pltpu.SemaphoreType.DMA((2,))],
    )
    return pl.pallas_call(
        functools.partial(_combine_kernel, ctx_tiles),
        grid_spec=grid_spec,
        out_shape=out_shape,
        compiler_params=pltpu.CompilerParams(
            dimension_semantics=("arbitrary",), vmem_limit_bytes=VMEM_LIMIT),
        name="moe_combine",
    )(tile_mod, slots, slots, x1, rw_t, mod, norm_final, ys)


def _moe(layer, h2, ri, rw, counts, x1, tile_mod, mod, norm_final, lw, ctx_tiles):
    n = h2.shape[0]
    n_tiles = n // TM
    n_slots = n * TOP_K + N_EXPERTS * TM
    cnt = counts[:, 0].astype(I32)
    padded = ((cnt + TM - 1) // TM) * TM
    ends = jnp.cumsum(padded)
    offsets = (ends - padded).astype(I32)
    n_used = (ends[-1] // TM).astype(I32)
    tile_ids = jnp.minimum(jnp.arange(n_slots // TM, dtype=I32), n_used - 1)
    tile_expert = jnp.sum((ends[None, :] // TM <= tile_ids[:, None]).astype(I32), axis=1)
    onehot = ri[0:TOP_K, :, None] == jnp.arange(N_EXPERTS, dtype=I32)
    slot = jnp.sum(jnp.where(onehot, offsets, 0), axis=-1) + ri[TOP_K:2 * TOP_K]
    slots = slot.reshape(TOP_K, n_tiles, TM).transpose(1, 0, 2).reshape(n_tiles, 1, TOP_K * TM)
    tail = n_used + jnp.arange(N_EXPERTS, dtype=I32)
    zero_tiles = jnp.concatenate([jnp.where(padded > 0, ends - TM, -1),
                                  jnp.where(tail < n_slots // TM, tail * TM, -1)]).astype(I32)

    prev_e = jnp.concatenate([jnp.full((1,), -1, I32), tile_expert[:-1]])
    first = (tile_expert != prev_e).astype(I32)
    next_tile = (ends // TM)[tile_expert]
    last = (tile_ids + 1 == next_tile).astype(I32)
    nxt = jnp.where(next_tile < n_used, tile_expert[jnp.minimum(next_tile, n_slots // TM - 1)], -1).astype(I32)
    parity = ((jnp.cumsum(first) - 1) & 1).astype(I32)
    ffn_tabs = (tile_expert, n_used.reshape(1), first, last, nxt, parity)

    xs = _dispatch(zero_tiles, slots, h2, n_slots)
    ys = _ffn(layer, ffn_tabs, xs, lw["w_gate_up"], lw["b_gate_up"], lw["w_down"], lw["b_down"])
    return _combine(tile_mod, slots, x1, rw.T, mod, norm_final, ys, ctx_tiles)


def _block_diag_pairs(w):
    z = jnp.zeros_like(w[:, 0::2])
    top = jnp.concatenate([w[:, 0::2], z], axis=-1)
    bot = jnp.concatenate([z, w[:, 1::2]], axis=-1)
    return jnp.concatenate([top, bot], axis=-2)


def kernel(x_prompt, x_sample, state_hgrn, state_rglru, c, c_ctx, norm_mix_w, norm_ffn_w, norm_final_w, w_mod, b_mod, w_in, hgrn_lb_logits, hgrn_norm_w, sconv_w, lru_conv_w, lru_conv_b, lru_wa, lru_ba, lru_wi, lru_bi, lru_lambda, w_branch, w_merge, b_merge, w_out, w_router, b_router, w_gate_up, b_gate_up, w_down, b_down):
    depth = w_in.shape[0]
    bp, tp, _ = x_prompt.shape
    bs, ts, _ = x_sample.shape
    assert tp == TM and ts % TM == 0 and bs + 1 <= MOD_ROWS
    n_ctx = bp * tp
    n_ctx_tiles = n_ctx // TM
    nt_lat = ts // TM

    x = jnp.concatenate([x_prompt.reshape(n_ctx, D_MODEL), x_sample.reshape(bs * ts, D_MODEL)], axis=0)
    n_tiles = x.shape[0] // TM

    lb_cum = jnp.cumsum(jax.nn.softmax(hgrn_lb_logits.astype(F32), axis=0), axis=0)
    lower_bounds = lb_cum - lb_cum[0:1]

    cc = jnp.zeros((MOD_ROWS, D_MODEL), F32).at[0].set(c_ctx).at[1:1 + bs].set(c)
    mod_all = _modulation(cc, w_mod, b_mod)
    tile_mod = jnp.concatenate([jnp.zeros((n_ctx_tiles,), I32),
                                1 + jnp.arange(bs * nt_lat, dtype=I32) // nt_lat])

    lat = jnp.arange(bs * nt_lat, dtype=I32)
    lat_b, lat_c = lat // nt_lat, lat % nt_lat
    ctx = jnp.arange(bp, dtype=I32)
    one = jnp.ones_like(ctx)
    tabs = (
        jnp.concatenate([ctx, n_ctx_tiles + lat]),
        jnp.concatenate([ctx, n_ctx_tiles + lat_b * nt_lat + (nt_lat - 1 - lat_c)]),
        jnp.concatenate([one, (lat_c == 0).astype(I32)]),
        jnp.concatenate([one, (lat_c == nt_lat - 1).astype(I32)]),
        jnp.concatenate([ctx, bp + lat_b]),
    )

    new_hgrn, new_lru = [], []
    for l in range(depth):
        lw = dict(
            norm_mix=norm_mix_w[l].reshape(1, D_MODEL), norm_ffn=norm_ffn_w[l].reshape(1, D_MODEL),
            w_merge=w_merge[l].astype(BF16), b_merge=b_merge[l].reshape(1, 3 * D_MODEL),
            w_branch=w_branch[l].astype(BF16), w_out=w_out[l].astype(BF16),
            hgrn_norm=hgrn_norm_w[l].reshape(1, W_MIX), sconv_w=sconv_w[l],
            w_router_t=w_router[l].T, b_router=b_router[l].reshape(N_EXPERTS, 1),
            w_gate_up=w_gate_up, b_gate_up=b_gate_up, w_down=w_down, b_down=b_down)
        mod = mod_all[l].reshape(MOD_ROWS, 1, N_MOD * D_MODEL)

        lru_par = jnp.concatenate(
            [lru_conv_w[l], lru_conv_b[l][None], lru_ba[l], lru_bi[l], lru_lambda[l],
             jnp.zeros((5, W_MIX), F32)], axis=0)
        p = _premix(x, tile_mod, mod, lw["norm_mix"], w_in[l].astype(BF16), lower_bounds[l], lru_par, n_ctx_tiles)
        w_lru = jnp.concatenate([_block_diag_pairs(lru_wa[l]), _block_diag_pairs(lru_wi[l])], axis=-1).astype(BF16)
        s0_hgrn = jnp.concatenate([jnp.zeros((bp,) + state_hgrn.shape[2:], F32), state_hgrn[:, l]], axis=0)
        s0_lru = jnp.concatenate([jnp.zeros((bp, 2, W_MIX), F32), state_rglru[:, l]], axis=0)
        o_f, hl_f, o_b, hl_b, s_hgrn, s_lru = _scans(p, tabs, lru_par, w_lru, s0_hgrn, s0_lru)
        new_hgrn.append(s_hgrn[:bp])
        new_lru.append(s_lru[:bp])

        x1, h2, ri, rw, counts = _merge(x, p, (o_f, hl_f, o_b, hl_b), tile_mod, mod, lw, n_ctx_tiles)
        x = _moe(l, h2, ri, rw, counts, x1, tile_mod, mod, norm_final_w.reshape(1, D_MODEL), lw,
                 ctx_tiles=(n_ctx_tiles if l == depth - 1 else 0))

    y_prompt = x[0].reshape(bp, tp, D_MODEL)
    y_sample = x[1].reshape(bs, ts, D_MODEL)
    return (y_prompt, y_sample, jnp.stack(new_hgrn, axis=1), jnp.stack(new_lru, axis=1))
```
